```python
import math
import jax
import jax.numpy as jnp
from jax import lax
import numpy as np

D_MODEL = 2048
BATCH = 4
SEQ = 2048
DEPTH = 2
DEC_BATCH = 128
DEC_SEQ = 1
PAST_LEN = 2048
PAGE_SIZE = 128

HEAD_DIM = 128
N_EVEN = (DEPTH + 1) // 2
N_ODD = DEPTH // 2
A_HEADS = 8
A_KV_HEADS = 2
A_GROUP = A_HEADS // A_KV_HEADS
CMP_BLOCK = 32
SEL_BLOCK = 64
N_SEL = 16
A_WINDOW = 512
B_HEADS = 8
B_KV_HEADS = 2
B_GROUP = B_HEADS // B_KV_HEADS
IDX_HEADS = 4
IDX_DIM = 64
IDX_TOPK = 256
C_HEADS = 8
DILATED_CFG = ((128, 1), (512, 4), (2048, 16))
C_WINDOW = 2048
R_HEADS = 4
R_DK = 128
R_DV = 256
R_CHUNK = 128
ROPE_BASE = 10000.0
N_BUCKETS = 32
BUCKET_MAX_DIST = 128
BIAS_HEADS = 16
D_FF = 5632
PLE_DIM = 256
QBLOCK = 128
RMS_EPS = 1e-6
NEG_INF = -1e30
FORCE_SCORE = 1e4

EVEN_SPLITS = (A_HEADS * HEAD_DIM, 3 * A_KV_HEADS * 2 * HEAD_DIM, A_HEADS * 3,
               B_HEADS * HEAD_DIM, B_KV_HEADS * 2 * HEAD_DIM, IDX_HEADS * IDX_DIM, IDX_DIM, IDX_HEADS)
EVEN_IN = 4444
EVEN_MIX = (A_HEADS + B_HEADS) * HEAD_DIM
ODD_SPLITS = (C_HEADS * HEAD_DIM, C_HEADS * 2 * HEAD_DIM, R_HEADS * R_DK, R_HEADS * R_DK,
              R_HEADS * R_DV, R_HEADS * R_DV)
ODD_IN = 6144
ODD_MIX = C_HEADS * HEAD_DIM + R_HEADS * R_DV

kernel_name = 'hybrid_nsa_dsa_dilated_retention_step'


def split_cols(z, sizes):
    offs = []
    acc = 0
    for s in sizes[:-1]:
        acc += s
        offs.append(acc)
    return jnp.split(z, offs, axis=-1)


def to_blocks(x):
    b, t = x.shape[:2]
    return x.reshape((b, t // QBLOCK, QBLOCK) + x.shape[2:]).swapaxes(0, 1)


def from_blocks(y):
    n, b, q = y.shape[:3]
    return y.swapaxes(0, 1).reshape((b, n * q) + y.shape[3:])


def rmsnorm(x, g):
    xf = x.astype(jnp.float32)
    y = xf * lax.rsqrt(jnp.mean(xf * xf, axis=-1, keepdims=True) + RMS_EPS)
    return (y * g.astype(jnp.float32)).astype(x.dtype)


def swiglu(h, w_in, w_out):
    g, u = jnp.split(h @ w_in, 2, axis=-1)
    return (jax.nn.silu(g) * u) @ w_out


def ffn_half(x, g, w_in, w_out):
    return x + 0.5 * swiglu(rmsnorm(x, g), w_in, w_out)


def ple_add(x, p, g, w_gate, w_proj):
    return x + jax.nn.sigmoid(rmsnorm(x, g) @ w_gate) * (p @ w_proj)


def t5_bucket(dist):
    dist = jnp.maximum(dist, 0)
    exact = N_BUCKETS // 2
    scaled = jnp.log(jnp.maximum(dist, exact).astype(jnp.float32) / exact) / math.log(BUCKET_MAX_DIST / exact)
    large = jnp.minimum(exact + (scaled * (N_BUCKETS - exact)).astype(jnp.int32), N_BUCKETS - 1)
    return jnp.where(dist < exact, dist, large)


def head_bias(tbl, dist, n_kv, n_grp):
    b = tbl[t5_bucket(dist)]
    return jnp.transpose(b.reshape(dist.shape + (n_kv, n_grp)), (0, 2, 3, 1))


def gathered_bias(tbl, dist, n_kv, n_grp):
    tbl_t = tbl.T.reshape(n_kv, n_grp, N_BUCKETS)
    g = jnp.arange(n_kv)[:, None, None]
    a = jnp.arange(n_grp)[None, :, None]
    return tbl_t[g, a, t5_bucket(dist)[:, :, :, None, :]]


def masked_softmax(logits, mask):
    logits = jnp.where(mask, logits, NEG_INF)
    m = jnp.max(logits, axis=-1, keepdims=True)
    e = jnp.where(mask, jnp.exp(logits - m), 0.0)
    return e / jnp.maximum(jnp.sum(e, axis=-1, keepdims=True), 1e-30)


def rope(x, pos):
    half = x.shape[-1] // 2
    inv = ROPE_BASE ** (-jnp.arange(half, dtype=jnp.float32) / half)
    ang = pos.astype(jnp.float32)[:, None] * inv[None, :]
    cos = jnp.cos(ang)[None, :, None, :]
    sin = jnp.sin(ang)[None, :, None, :]
    xf = x.astype(jnp.float32)
    x1, x2 = xf[..., :half], xf[..., half:]
    return jnp.concatenate([x1 * cos - x2 * sin, x1 * sin + x2 * cos], axis=-1).astype(x.dtype)


def gather_past(pool, page_table):
    rows = pool[page_table]
    return rows.reshape((page_table.shape[0], -1) + pool.shape[2:])


def fetch_paged(pool, page_table, new_rows, pos, head):
    nb = pos.shape[0]
    b = jnp.arange(nb).reshape((nb,) + (1,) * (pos.ndim - 1))
    past = jnp.clip(pos, 0, PAST_LEN - 1)
    phys = page_table[b, past // PAGE_SIZE] * PAGE_SIZE + past % PAGE_SIZE
    flat = pool.reshape((-1,) + pool.shape[2:])
    loc = jnp.clip(pos - PAST_LEN, 0, new_rows.shape[1] - 1)
    if head is None:
        from_past = flat[phys]
        from_new = new_rows[b, loc]
    else:
        from_past = flat[phys, head]
        from_new = new_rows[b, loc, head]
    cond = (pos >= PAST_LEN).reshape(pos.shape + (1,) * (from_past.ndim - pos.ndim))
    return jnp.where(cond, from_new, from_past)


def nsa_compress(rows, pe, w):
    b, l = rows.shape[:2]
    nb = l // CMP_BLOCK
    blk = rows[:, :nb * CMP_BLOCK].reshape(b, nb, CMP_BLOCK, A_KV_HEADS, 2, HEAD_DIM)
    blk = blk + jnp.transpose(pe, (1, 0, 2))[:, None]
    return jnp.einsum('bnlgcd,clde->bngce', blk, w)


def nsa_attend(q, gate, q_pos, cmp_kv, fetch_sel, n_sel_blocks, win_kv, win_pos, rel_bias):
    b, tq = q.shape[:2]
    tbl = rel_bias[:, :A_HEADS]
    n_cmp = cmp_kv.shape[1]
    dist_c = q_pos[:, None] - (jnp.arange(n_cmp) * CMP_BLOCK + CMP_BLOCK - 1)[None, :]
    logit_c = jnp.einsum('btgad,bngd->btgan', q, cmp_kv[..., 0, :]).astype(jnp.float32) + head_bias(tbl, dist_c, A_KV_HEADS, A_GROUP)
    p_c = masked_softmax(logit_c, (dist_c >= 0)[None, :, None, None, :])
    o_c = jnp.einsum('btgan,bngd->btgad', p_c.astype(q.dtype), cmp_kv[..., 1, :])
    ratio = SEL_BLOCK // CMP_BLOCK
    score = jnp.pad(p_c.sum(axis=3), ((0, 0), (0, 0), (0, 0), (0, n_sel_blocks * ratio - n_cmp)))
    score = score.reshape(b, tq, A_KV_HEADS, n_sel_blocks, ratio).sum(-1)
    blk = jnp.arange(n_sel_blocks)[None, :]
    cur = (q_pos // SEL_BLOCK)[:, None]
    forced = (blk == 0) | (blk == cur) | (blk == cur - 1)
    admissible = blk * SEL_BLOCK <= q_pos[:, None]
    score = jnp.where(forced[None, :, None, :], FORCE_SCORE, score)
    score = jnp.where(admissible[None, :, None, :], score, NEG_INF)
    n_take = min(N_SEL, n_sel_blocks)
    top_s, top_i = lax.top_k(score, n_take)
    rows = top_i[..., None] * SEL_BLOCK + jnp.arange(SEL_BLOCK)
    kv_s = fetch_sel(rows).reshape(b, tq, A_KV_HEADS, n_take * SEL_BLOCK, 2, HEAD_DIM)
    dist_s = q_pos[None, :, None, None] - rows.reshape(b, tq, A_KV_HEADS, n_take * SEL_BLOCK)
    ok_s = (dist_s >= 0) & jnp.repeat(top_s > 0.5 * NEG_INF, SEL_BLOCK, axis=-1)
    logit_s = jnp.einsum('btgad,btgsd->btgas', q, kv_s[..., 0, :]).astype(jnp.float32) + gathered_bias(tbl, dist_s, A_KV_HEADS, A_GROUP)
    p_s = masked_softmax(logit_s, ok_s[:, :, :, None, :])
    o_s = jnp.einsum('btgas,btgsd->btgad', p_s.astype(q.dtype), kv_s[..., 1, :])
    dist_w = q_pos[:, None] - win_pos[None, :]
    ok_w = (dist_w >= 0) & (dist_w <= A_WINDOW) & (win_pos >= 0)[None, :]
    logit_w = jnp.einsum('btgad,bsgd->btgas', q, win_kv[..., 0, :]).astype(jnp.float32) + head_bias(tbl, dist_w, A_KV_HEADS, A_GROUP)
    p_w = masked_softmax(logit_w, ok_w[None, :, None, None, :])
    o_w = jnp.einsum('btgas,bsgd->btgad', p_w.astype(q.dtype), win_kv[..., 1, :])
    out = gate[..., 0:1] * o_c + gate[..., 1:2] * o_s + gate[..., 2:3] * o_w
    return out.reshape(b, tq, A_HEADS * HEAD_DIM)


def dsa_attend(q, q_pos, qi, wi, k_idx, k_pos, fetch_kv, n_take, rel_bias):
    b, tq = q.shape[:2]
    rel = jax.nn.relu(jnp.einsum('bthi,bsi->bths', qi, k_idx).astype(jnp.float32))
    score = jnp.einsum('bth,bths->bts', wi.astype(jnp.float32), rel)
    score = jnp.where((k_pos[None, :] <= q_pos[:, None])[None], score, NEG_INF)
    top_s, top_i = lax.top_k(score, n_take)
    sel_pos = k_pos[top_i]
    kv = fetch_kv(sel_pos)
    tbl = rel_bias[:, A_HEADS:A_HEADS + B_HEADS]
    bias = tbl[t5_bucket(q_pos[None, :, None] - sel_pos)].reshape(b, tq, n_take, B_KV_HEADS, B_GROUP)
    logits = jnp.einsum('btgad,btkgd->btgak', q, kv[..., 0, :]).astype(jnp.float32) + jnp.transpose(bias, (0, 1, 3, 4, 2))
    p = masked_softmax(logits, (top_s > 0.5 * NEG_INF)[:, :, None, None, :])
    o = jnp.einsum('btgak,btkgd->btgad', p.astype(q.dtype), kv[..., 1, :])
    return o.reshape(b, tq, B_HEADS * HEAD_DIM)


def dilated_attend(q, q_pos, fetch_kv, rel_bias):
    b, tq = q.shape[:2]
    tbl = rel_bias[:, :C_HEADS]
    outs, lses = [], []
    for window, dil in DILATED_CFG:
        dist = dil * jnp.arange(window // dil + 1)
        pos = q_pos[:, None] - dist[None, :]
        kv = fetch_kv(pos)
        logits = jnp.einsum('bthd,btjhd->bthj', q, kv[..., 0, :]).astype(jnp.float32) + tbl[t5_bucket(dist)].T
        logits = jnp.where((pos >= 0)[None, :, None, :], logits, NEG_INF)
        lse = jax.nn.logsumexp(logits, axis=-1, keepdims=True)
        outs.append(jnp.einsum('bthj,btjhd->bthd', jnp.exp(logits - lse).astype(q.dtype), kv[..., 1, :]))
        lses.append(lse)
    wgt = jax.nn.softmax(jnp.stack(lses), axis=0)
    out = jnp.sum(wgt * jnp.stack(outs), axis=0)
    return out.astype(q.dtype).reshape(b, tq, C_HEADS * HEAD_DIM)


def retention_log_decay():
    return jnp.log1p(-jnp.exp2(-5.0 - jnp.arange(R_HEADS, dtype=jnp.float32)))


def retention_chunk(state, q, k, v):
    c = q.shape[1]
    log_g = retention_log_decay()
    i = jnp.arange(c)
    rel = i[:, None] - i[None, :]
    decay = jnp.where(rel >= 0, jnp.exp(log_g[:, None, None] * jnp.maximum(rel, 0).astype(jnp.float32)), 0.0)
    qf, kf, vf = q.astype(jnp.float32), k.astype(jnp.float32), v.astype(jnp.float32)
    scores = jnp.einsum('bihd,bjhd->bhij', qf, kf) * decay[None]
    out = jnp.einsum('bhij,bjhe->bihe', scores, vf)
    xi = jnp.exp(log_g[None, :] * (i + 1).astype(jnp.float32)[:, None])
    out = out + jnp.einsum('bihd,bhde->bihe', qf, state) * xi[None, :, :, None]
    zeta = jnp.exp(log_g[None, :] * (c - 1 - i).astype(jnp.float32)[:, None])
    new_state = state * jnp.exp(log_g * c)[None, :, None, None] + jnp.einsum('bjhd,bjhe->bhde', kf * zeta[None, :, :, None], vf)
    return new_state, out


def retention_readout(o, gate, gn):
    b, t = o.shape[:2]
    mu = jnp.mean(o, axis=-1, keepdims=True)
    var = jnp.mean(jnp.square(o - mu), axis=-1, keepdims=True)
    o = ((o - mu) * lax.rsqrt(var + RMS_EPS)).reshape(b, t, R_HEADS * R_DV)
    return (jax.nn.silu(gate.astype(jnp.float32)) * o * gn.astype(jnp.float32)).astype(gate.dtype)


def project_even(h, w_in):
    b, t, _ = h.shape
    qa, kva, ga, qb, kvb, qi, ki, wi = split_cols(h @ w_in, EVEN_SPLITS)
    qa = qa.reshape(b, t, A_KV_HEADS, A_GROUP, HEAD_DIM) * HEAD_DIM ** -0.5
    kva = kva.reshape(b, t, 3, A_KV_HEADS, 2, HEAD_DIM)
    ga = jax.nn.sigmoid(ga.reshape(b, t, A_KV_HEADS, A_GROUP, 3))
    qb = qb.reshape(b, t, B_KV_HEADS, B_GROUP, HEAD_DIM) * HEAD_DIM ** -0.5
    kvb = kvb.reshape(b, t, B_KV_HEADS, 2, HEAD_DIM)
    qi = qi.reshape(b, t, IDX_HEADS, IDX_DIM)
    wi = wi * IDX_HEADS ** -0.5
    return qa, kva, ga, qb, kvb, qi, ki, wi


def project_odd(h, w_in):
    b, t, _ = h.shape
    qc, kvc, qr, kr, vr, gr = split_cols(h @ w_in, ODD_SPLITS)
    qc = qc.reshape(b, t, C_HEADS, HEAD_DIM) * HEAD_DIM ** -0.5
    kvc = kvc.reshape(b, t, C_HEADS, 2, HEAD_DIM)
    return (qc, kvc, qr.reshape(b, t, R_HEADS, R_DK), kr.reshape(b, t, R_HEADS, R_DK),
            vr.reshape(b, t, R_HEADS, R_DV), gr)


def even_mixer_prompt(h, w_in, w_out, cmp_pe, cmp_w, rel_bias):
    b, t, _ = h.shape
    qa, kva, ga, qb, kvb, qi, ki, wi = project_even(h, w_in)
    kv_cmp, kv_sel, kv_win = kva[:, :, 0], kva[:, :, 1], kva[:, :, 2]
    cmp_sum = nsa_compress(kv_cmp, cmp_pe, cmp_w)
    n_sel_blocks = -(-t // SEL_BLOCK)
    n_take_b = min(IDX_TOPK, t // 4)
    win_pad = jnp.pad(kv_win, ((0, 0), (A_WINDOW, 0), (0, 0), (0, 0), (0, 0)))
    b_sel = jnp.arange(b)[:, None, None, None, None]
    g_sel = jnp.arange(A_KV_HEADS)[None, None, :, None, None]
    b_idx = jnp.arange(b)[:, None, None]
    k_pos = jnp.arange(t)

    def fetch_sel(rows):
        return kv_sel[b_sel, jnp.clip(rows, 0, t - 1), g_sel]

    def fetch_b(pos):
        return kvb[b_idx, pos]

    def block(args):
        t0, qa_b, ga_b, qb_b, qi_b, wi_b = args
        q_pos = t0 + jnp.arange(QBLOCK)
        win_kv = lax.dynamic_slice_in_dim(win_pad, t0, QBLOCK + A_WINDOW, axis=1)
        win_pos = t0 - A_WINDOW + jnp.arange(QBLOCK + A_WINDOW)
        oa = nsa_attend(qa_b, ga_b, q_pos, cmp_sum, fetch_sel, n_sel_blocks, win_kv, win_pos, rel_bias)
        ob = dsa_attend(qb_b, q_pos, qi_b, wi_b, ki, k_pos, fetch_b, n_take_b, rel_bias)
        return jnp.concatenate([oa, ob], axis=-1)

    starts = jnp.arange(t // QBLOCK) * QBLOCK
    out = lax.map(block, (starts, to_blocks(qa), to_blocks(ga), to_blocks(qb), to_blocks(qi), to_blocks(wi)))
    out = from_blocks(out) @ w_out
    return out, (kv_cmp, kv_sel, kvb, ki, kv_win[:, t - min(A_WINDOW, t):])


def even_mixer_sample(h, pool_cmp, pool_sel, pool_b, pool_idx, win_buf, page_table, w_in, w_out, cmp_pe, cmp_w, rel_bias):
    b, tn, _ = h.shape
    l = PAST_LEN + tn
    qa, kva, ga, qb, kvb, qi, ki, wi = project_even(h, w_in)
    kv_cmp, kv_sel, kv_win = kva[:, :, 0], kva[:, :, 1], kva[:, :, 2]
    q_pos = PAST_LEN + jnp.arange(tn)
    cmp_sum = jnp.concatenate([nsa_compress(gather_past(pool_cmp, page_table), cmp_pe, cmp_w),
                               nsa_compress(kv_cmp, cmp_pe, cmp_w)], axis=1)
    g_sel = jnp.arange(A_KV_HEADS)[None, None, :, None, None]

    def fetch_sel(rows):
        return fetch_paged(pool_sel, page_table, kv_sel, rows, g_sel)

    def fetch_b(pos):
        return fetch_paged(pool_b, page_table, kvb, pos, None)

    w_eff = win_buf.shape[1]
    win_kv = jnp.concatenate([win_buf, kv_win], axis=1)
    win_pos = PAST_LEN - w_eff + jnp.arange(w_eff + tn)
    oa = nsa_attend(qa, ga, q_pos, cmp_sum, fetch_sel, -(-l // SEL_BLOCK), win_kv, win_pos, rel_bias)
    k_idx = jnp.concatenate([gather_past(pool_idx, page_table), ki], axis=1)
    ob = dsa_attend(qb, q_pos, qi, wi, k_idx, jnp.arange(l), fetch_b, min(IDX_TOPK, l // 4), rel_bias)
    out = jnp.concatenate([oa, ob], axis=-1) @ w_out
    return out, (kv_cmp, kv_sel, kvb, ki, kv_win)


def odd_mixer_prompt(h, w_in, w_out, gn, rel_bias):
    b, t, _ = h.shape
    qc, kvc, qr, kr, vr, gr = project_odd(h, w_in)

    def fetch_c(pos):
        return kvc[:, jnp.clip(pos, 0, t - 1)]

    def block(args):
        t0, q_b = args
        return dilated_attend(q_b, t0 + jnp.arange(QBLOCK), fetch_c, rel_bias)

    starts = jnp.arange(t // QBLOCK) * QBLOCK
    oc = from_blocks(lax.map(block, (starts, to_blocks(qc))))
    pos = jnp.arange(t)
    qr = rope(qr, pos)
    kr = rope(kr, pos) * R_DK ** -0.5
    n_ch = t // R_CHUNK

    def chunks(x):
        return x.reshape((b, n_ch, R_CHUNK) + x.shape[2:]).swapaxes(0, 1)

    s0 = jnp.zeros((b, R_HEADS, R_DK, R_DV), jnp.float32)
    s_final, o = lax.scan(lambda s, xs: retention_chunk(s, xs[0], xs[1], xs[2]), s0, (chunks(qr), chunks(kr), chunks(vr)))
    o = o.swapaxes(0, 1).reshape(b, t, R_HEADS, R_DV)
    od = retention_readout(o, gr, gn)
    out = jnp.concatenate([oc, od], axis=-1) @ w_out
    return out, (kvc[:, t - min(C_WINDOW, t):], s_final)


def odd_mixer_sample(h, c_buf, r_state, w_in, w_out, gn, rel_bias):
    b, tn, _ = h.shape
    qc, kvc, qr, kr, vr, gr = project_odd(h, w_in)
    q_pos = PAST_LEN + jnp.arange(tn)
    w_eff = c_buf.shape[1]
    start = PAST_LEN - w_eff

    def fetch_c(pos):
        from_buf = c_buf[:, jnp.clip(pos - start, 0, w_eff - 1)]
        from_new = kvc[:, jnp.clip(pos - PAST_LEN, 0, tn - 1)]
        return jnp.where((pos >= PAST_LEN)[None, :, :, None, None, None], from_new, from_buf)

    oc = dilated_attend(qc, q_pos, fetch_c, rel_bias)
    qr = rope(qr, q_pos)
    kr = rope(kr, q_pos) * R_DK ** -0.5
    s_new, o = retention_chunk(r_state.astype(jnp.float32), qr, kr, vr)
    od = retention_readout(o, gr, gn)
    out = jnp.concatenate([oc, od], axis=-1) @ w_out
    return out, (kvc, s_new)


def stack_state(states, k):
    return jnp.stack([s[k] for s in states])


def setup_inputs(seed: int = 0) -> dict:
    key = jax.random.key(seed)
    ks = iter(jax.random.split(key, 32))

    def nrm(shape, scale):
        return scale * jax.random.normal(next(ks), shape, jnp.float32)

    n_pages = PAST_LEN // PAGE_SIZE
    n_used = DEC_BATCH * n_pages
    n_pool = n_used + n_used // 4
    page_table = jax.random.permutation(next(ks), n_pool)[:n_used].reshape(DEC_BATCH, n_pages).astype(jnp.int32)
    w_a = min(A_WINDOW, PAST_LEN)
    w_c = min(C_WINDOW, PAST_LEN)
    return {
        'x_prompt': nrm((BATCH, SEQ, D_MODEL), 1.0),
        'x_sample': nrm((DEC_BATCH, DEC_SEQ, D_MODEL), 1.0),
        'cache_a_cmp': nrm((N_EVEN, n_pool, PAGE_SIZE, A_KV_HEADS, 2, HEAD_DIM), 1.0),
        'cache_a_sel': nrm((N_EVEN, n_pool, PAGE_SIZE, A_KV_HEADS, 2, HEAD_DIM), 1.0),
        'cache_b_kv': nrm((N_EVEN, n_pool, PAGE_SIZE, B_KV_HEADS, 2, HEAD_DIM), 1.0),
        'cache_b_idx': nrm((N_EVEN, n_pool, PAGE_SIZE, IDX_DIM), 1.0),
        'state_a_win': nrm((N_EVEN, DEC_BATCH, w_a, A_KV_HEADS, 2, HEAD_DIM), 1.0),
        'state_c_win': nrm((N_ODD, DEC_BATCH, w_c, C_HEADS, 2, HEAD_DIM), 1.0),
        'state_ret': nrm((N_ODD, DEC_BATCH, R_HEADS, R_DK, R_DV), 0.3),
        'page_table': page_table,
        'p_prompt': nrm((DEPTH, BATCH, SEQ, PLE_DIM), 1.0),
        'p_sample': nrm((DEPTH, DEC_BATCH, DEC_SEQ, PLE_DIM), 1.0),
        'rel_bias': nrm((N_BUCKETS, BIAS_HEADS), 0.2),
        'norm_g': 1.0 + nrm((DEPTH, 4, D_MODEL), 0.05),
        'final_norm': 1.0 + nrm((D_MODEL,), 0.05),
        'ffn_w_in': nrm((DEPTH, 2, D_MODEL, 2 * D_FF), D_MODEL ** -0.5),
        'ffn_w_out': nrm((DEPTH, 2, D_FF, D_MODEL), D_FF ** -0.5),
        'ple_gate': nrm((DEPTH, D_MODEL, D_MODEL), D_MODEL ** -0.5),
        'ple_proj': nrm((DEPTH, PLE_DIM, D_MODEL), PLE_DIM ** -0.5),
        'even_w_in': nrm((N_EVEN, D_MODEL, EVEN_IN), D_MODEL ** -0.5),
        'even_w_out': nrm((N_EVEN, EVEN_MIX, D_MODEL), EVEN_MIX ** -0.5),
        'nsa_cmp_pe': nrm((N_EVEN, 2, CMP_BLOCK, HEAD_DIM), 0.1),
        'nsa_cmp_w': nrm((N_EVEN, 2, CMP_BLOCK, HEAD_DIM, HEAD_DIM), (CMP_BLOCK * HEAD_DIM) ** -0.5),
        'odd_w_in': nrm((N_ODD, D_MODEL, ODD_IN), D_MODEL ** -0.5),
        'odd_w_out': nrm((N_ODD, ODD_MIX, D_MODEL), ODD_MIX ** -0.5),
        'ret_gn': 1.0 + nrm((N_ODD, R_HEADS * R_DV), 0.05),
    }


def reference(x_prompt, x_sample, cache_a_cmp, cache_a_sel, cache_b_kv, cache_b_idx, state_a_win, state_c_win,
              state_ret, page_table, p_prompt, p_sample, rel_bias, norm_g, final_norm, ffn_w_in, ffn_w_out,
              ple_gate, ple_proj, even_w_in, even_w_out, nsa_cmp_pe, nsa_cmp_w, odd_w_in, odd_w_out, ret_gn):
    xp, xs = x_prompt, x_sample
    ev_p, ev_s, od_p, od_s = [], [], [], []
    for i in range(DEPTH):
        xp = ffn_half(xp, norm_g[i, 0], ffn_w_in[i, 0], ffn_w_out[i, 0])
        xs = ffn_half(xs, norm_g[i, 0], ffn_w_in[i, 0], ffn_w_out[i, 0])
        hp = rmsnorm(xp, norm_g[i, 1])
        hs = rmsnorm(xs, norm_g[i, 1])
        if i % 2 == 0:
            e = i // 2
            mp, sp = even_mixer_prompt(hp, even_w_in[e], even_w_out[e], nsa_cmp_pe[e], nsa_cmp_w[e], rel_bias)
            ms, ss = even_mixer_sample(hs, cache_a_cmp[e], cache_a_sel[e], cache_b_kv[e], cache_b_idx[e],
                                       state_a_win[e], page_table, even_w_in[e], even_w_out[e],
                                       nsa_cmp_pe[e], nsa_cmp_w[e], rel_bias)
            ev_p.append(sp)
            ev_s.append(ss)
        else:
            o = i // 2
            mp, sp = odd_mixer_prompt(hp, odd_w_in[o], odd_w_out[o], ret_gn[o], rel_bias)
            ms, ss = odd_mixer_sample(hs, state_c_win[o], state_ret[o], odd_w_in[o], odd_w_out[o], ret_gn[o], rel_bias)
            od_p.append(sp)
            od_s.append(ss)
        xp = xp + mp
        xs = xs + ms
        xp = ffn_half(xp, norm_g[i, 2], ffn_w_in[i, 1], ffn_w_out[i, 1])
        xs = ffn_half(xs, norm_g[i, 2], ffn_w_in[i, 1], ffn_w_out[i, 1])
        xp = ple_add(xp, p_prompt[i], norm_g[i, 3], ple_gate[i], ple_proj[i])
        xs = ple_add(xs, p_sample[i], norm_g[i, 3], ple_gate[i], ple_proj[i])
    y_prompt = rmsnorm(xp, final_norm)
    y_sample = rmsnorm(xs, final_norm)
    return (y_prompt, y_sample,
            stack_state(ev_p, 0), stack_state(ev_s, 0),
            stack_state(ev_p, 1), stack_state(ev_s, 1),
            stack_state(ev_p, 2), stack_state(ev_s, 2),
            stack_state(ev_p, 3), stack_state(ev_s, 3),
            stack_state(ev_p, 4), stack_state(ev_s, 4),
            stack_state(od_p, 0), stack_state(od_s, 0),
            stack_state(od_p, 1), stack_state(od_s, 1))
```

```python
import functools
import math

import numpy as np
import jax
import jax.numpy as jnp
from jax import lax
from jax.experimental import pallas as pl
from jax.experimental.pallas import tpu as pltpu

F32 = jnp.float32
BF16 = jnp.bfloat16
HI = lax.Precision.HIGHEST

D_MODEL = 2048
BATCH = 4
SEQ = 2048
DEC_BATCH = 128
PAST_LEN = 2048
PAGE = 128
N_PAGES = PAST_LEN // PAGE
HD = 128
D_FF = 5632
PLE_DIM = 256
N_TOK_P = BATCH * SEQ
N_TOK = N_TOK_P + DEC_BATCH
QB = 128
NQB = SEQ // QB
CMP_BLOCK = 32
N_CMP = SEQ // CMP_BLOCK
SEL_BLOCK = 64
N_SEL = 16
A_WINDOW = 512
IDX_TOPK = 256
R_HEADS = 4
R_DK = 128
R_DV = 256
N_BUCKETS = 32
RMS_EPS = 1e-6
NEG_INF = -1e30
BIG = 3.0e38
FORCE_SCORE = 1e4
Q_SCALE = HD ** -0.5
RK_SCALE = R_DK ** -0.5
EVEN_W = 4480
ODD_W = 6144
VMEM_LIMIT = 56 * 1024 * 1024
BISECT_STEPS = 10

E_QA, E_QB, E_CMP, E_SEL, E_WIN, E_KVB, E_QI, E_SM = 0, 1024, 2048, 2560, 3072, 3584, 4096, 4352
SM_KI, SM_WI, SM_GA = 0, 64, 68
O_QC, O_KVC, O_QR, O_KR, O_VR, O_GR = 0, 1024, 3072, 3584, 4096, 5120

LOG_G = [math.log1p(-(2.0 ** (-5.0 - h))) for h in range(R_HEADS)]


def _cparams(sem=None):
    return pltpu.CompilerParams(dimension_semantics=sem, vmem_limit_bytes=VMEM_LIMIT)


def _dot(a, b, precision=None):
    return jnp.dot(a, b, preferred_element_type=F32, precision=precision)


def _dot_nt(a, b, precision=None):
    return lax.dot_general(a, b, (((1,), (1,)), ((), ())), preferred_element_type=F32, precision=precision)


def _sigmoid(x):
    return 1.0 / (1.0 + jnp.exp(-x))


def _rms(x, g):
    return x * lax.rsqrt(jnp.mean(x * x, axis=-1, keepdims=True) + RMS_EPS) * g


FFN_TM, FFN_TF = 640, 512


def _ffn_kernel(x_ref, g_ref, wg_ref, wu_ref, wo_ref, o_ref, h_scr, acc_scr):
    j = pl.program_id(1)

    @pl.when(j == 0)
    def _():
        h_scr[...] = _rms(x_ref[...], g_ref[...]).astype(BF16)
        acc_scr[...] = jnp.zeros_like(acc_scr)

    h = h_scr[...]
    gate = _dot(h, wg_ref[...])
    up = _dot(h, wu_ref[...])
    act = gate * _sigmoid(gate) * up
    acc_scr[...] += _dot(act.astype(BF16), wo_ref[...])

    @pl.when(j == pl.num_programs(1) - 1)
    def _():
        o_ref[...] = x_ref[...] + 0.5 * acc_scr[...]


def _ffn_half(x, g, w_in, w_out):
    nj = D_FF // FFN_TF
    return pl.pallas_call(
        _ffn_kernel,
        grid=(N_TOK // FFN_TM, nj),
        in_specs=[
            pl.BlockSpec((FFN_TM, D_MODEL), lambda i, j: (i, 0)),
            pl.BlockSpec((1, D_MODEL), lambda i, j: (0, 0)),
            pl.BlockSpec((D_MODEL, FFN_TF), lambda i, j: (0, j)),
            pl.BlockSpec((D_MODEL, FFN_TF), lambda i, j: (0, j + nj)),
            pl.BlockSpec((FFN_TF, D_MODEL), lambda i, j: (j, 0)),
        ],
        out_specs=pl.BlockSpec((FFN_TM, D_MODEL), lambda i, j: (i, 0)),
        out_shape=jax.ShapeDtypeStruct((N_TOK, D_MODEL), F32),
        scratch_shapes=[pltpu.VMEM((FFN_TM, D_MODEL), BF16), pltpu.VMEM((FFN_TM, D_MODEL), F32)],
        compiler_params=_cparams(("parallel", "arbitrary")),
        name="ffn_half",
    )(x, g.reshape(1, D_MODEL), w_in, w_in, w_out)


PROJ_TM = 640


def _proj_in_kernel(x_ref, g_ref, w_ref, z_ref, zb_ref, h_scr):
    @pl.when(pl.program_id(1) == 0)
    def _():
        h_scr[...] = _rms(x_ref[...], g_ref[...]).astype(BF16)

    z = _dot(h_scr[...], w_ref[...])
    z_ref[...] = z
    zb_ref[...] = z.astype(BF16)


def _proj_in(x, g, w, tn):
    n = w.shape[1]
    return pl.pallas_call(
        _proj_in_kernel,
        grid=(N_TOK // PROJ_TM, n // tn),
        in_specs=[
            pl.BlockSpec((PROJ_TM, D_MODEL), lambda i, j: (i, 0)),
            pl.BlockSpec((1, D_MODEL), lambda i, j: (0, 0)),
            pl.BlockSpec((D_MODEL, tn), lambda i, j: (0, j)),
        ],
        out_specs=[pl.BlockSpec((PROJ_TM, tn), lambda i, j: (i, j)),
                   pl.BlockSpec((PROJ_TM, tn), lambda i, j: (i, j))],
        out_shape=[jax.ShapeDtypeStruct((N_TOK, n), F32), jax.ShapeDtypeStruct((N_TOK, n), BF16)],
        scratch_shapes=[pltpu.VMEM((PROJ_TM, D_MODEL), BF16)],
        compiler_params=_cparams(("parallel", "arbitrary")),
        name="proj_in",
    )(x, g.reshape(1, D_MODEL), w)


OUT_TM = 320


def _proj_out_kernel(x_ref, a_ref, b_ref, wa_ref, wb_ref, o_ref):
    o_ref[...] = (x_ref[...] + _dot(a_ref[...].astype(BF16), wa_ref[...])
                  + _dot(b_ref[...].astype(BF16), wb_ref[...]))


def _proj_out(x, a, b, wa, wb):
    ka, kb = a.shape[1], b.shape[1]
    return pl.pallas_call(
        _proj_out_kernel,
        grid=(N_TOK // OUT_TM,),
        in_specs=[
            pl.BlockSpec((OUT_TM, D_MODEL), lambda i: (i, 0)),
            pl.BlockSpec((OUT_TM, ka), lambda i: (i, 0)),
            pl.BlockSpec((OUT_TM, kb), lambda i: (i, 0)),
            pl.BlockSpec((ka, D_MODEL), lambda i: (0, 0)),
            pl.BlockSpec((kb, D_MODEL), lambda i: (0, 0)),
        ],
        out_specs=pl.BlockSpec((OUT_TM, D_MODEL), lambda i: (i, 0)),
        out_shape=jax.ShapeDtypeStruct((N_TOK, D_MODEL), F32),
        compiler_params=_cparams(("parallel",)),
        name="proj_out",
    )(x, a, b, wa, wb)


def _ple_kernel(x_ref, p_ref, g_ref, wg_ref, wp_ref, gf_ref, o_ref, *, final):
    x = x_ref[...]
    gate = _sigmoid(_dot(_rms(x, g_ref[...]).astype(BF16), wg_ref[...]))
    y = x + gate * _dot(p_ref[...].astype(BF16), wp_ref[...])
    o_ref[...] = _rms(y, gf_ref[...]) if final else y


def _ple(x, p, g, w_gate, w_proj, g_final, final):
    return pl.pallas_call(
        functools.partial(_ple_kernel, final=final),
        grid=(N_TOK // OUT_TM,),
        in_specs=[
            pl.BlockSpec((OUT_TM, D_MODEL), lambda i: (i, 0)),
            pl.BlockSpec((OUT_TM, PLE_DIM), lambda i: (i, 0)),
            pl.BlockSpec((1, D_MODEL), lambda i: (0, 0)),
            pl.BlockSpec((D_MODEL, D_MODEL), lambda i: (0, 0)),
            pl.BlockSpec((PLE_DIM, D_MODEL), lambda i: (0, 0)),
            pl.BlockSpec((1, D_MODEL), lambda i: (0, 0)),
        ],
        out_specs=pl.BlockSpec((OUT_TM, D_MODEL), lambda i: (i, 0)),
        out_shape=jax.ShapeDtypeStruct((N_TOK, D_MODEL), F32),
        compiler_params=_cparams(("parallel",)),
        name="ple_add",
    )(x, p, g.reshape(1, D_MODEL), w_gate, w_proj, g_final.reshape(1, D_MODEL))


def _t5_bucket_np(dist):
    dist = np.maximum(np.asarray(dist, np.int64), 0)
    exact = N_BUCKETS // 2
    out = {}
    for dt in (np.float32, np.float64):
        scaled = np.log(np.maximum(dist, exact).astype(dt) / dt(exact)) / dt(math.log(128 / exact))
        large = np.minimum(exact + (scaled * dt(N_BUCKETS - exact)).astype(np.int64), N_BUCKETS - 1)
        out[dt] = np.where(dist < exact, dist, large)
    assert np.array_equal(out[np.float32], out[np.float64])
    return out[np.float32].astype(np.int32)


def _cmp_block_of_col(col):
    return 2 * (col % 32) + col // 32


def _bias_kernel(tbl_ref, bk_near, bk_cmp, bk_dec, bk_decc, bk_dil, o_near, o_cmp, o_dec, o_decc, o_dil):
    def lookup(bucket, h):
        acc = jnp.zeros(bucket.shape, F32)
        for b in range(N_BUCKETS):
            acc = jnp.where(bucket == b, tbl_ref[b, h], acc)
        return acc

    for rel in range(3):
        bk = bk_near[rel]
        for h in range(16):
            o_near[rel, h] = lookup(bk, h)
    bk = bk_cmp[...]
    for h in range(8):
        o_cmp[h] = lookup(bk, h)
    bk = bk_dec[...]
    for h in range(16):
        o_dec[h:h + 1, :] = lookup(bk, h)
    bk = bk_decc[...]
    for h in range(8):
        o_decc[h:h + 1, :] = lookup(bk, h)
    lane = lax.broadcasted_iota(jnp.int32, (QB, 8), 1)
    for cfg in range(3):
        bk = bk_dil[cfg]
        acc = jnp.zeros((QB, 8), F32)
        for h in range(8):
            acc = jnp.where(lane == h, lookup(bk, h), acc)
        o_dil[cfg] = acc


DEC_W = 2176


def _bias_tables(rel_bias):
    r = np.arange(QB)[:, None]
    c = np.arange(QB)[None, :]
    bk_near = np.stack([_t5_bucket_np(rel * QB + r - c) for rel in range(3)])
    t = np.arange(SEQ)[:, None]
    col = np.arange(N_CMP)[None, :]
    bk_cmp = _t5_bucket_np(t - (CMP_BLOCK * _cmp_block_of_col(col) + CMP_BLOCK - 1))
    bk_dec = _t5_bucket_np(PAST_LEN - np.arange(DEC_W)[None, :])
    bk_decc = _t5_bucket_np(PAST_LEN - (CMP_BLOCK * _cmp_block_of_col(col) + CMP_BLOCK - 1))
    i = np.arange(QB)[:, None]
    bk_dil = np.stack([np.broadcast_to(_t5_bucket_np(dil * (QB - i)), (QB, 8)) for dil in (1, 4, 16)])
    vm = pl.BlockSpec(memory_space=pltpu.VMEM)
    return pl.pallas_call(
        _bias_kernel,
        in_specs=[pl.BlockSpec(memory_space=pltpu.SMEM), vm, vm, vm, vm, vm],
        out_specs=[vm] * 5,
        out_shape=[
            jax.ShapeDtypeStruct((3, 16, QB, QB), F32),
            jax.ShapeDtypeStruct((8, SEQ, N_CMP), F32),
            jax.ShapeDtypeStruct((16, DEC_W), F32),
            jax.ShapeDtypeStruct((8, N_CMP), F32),
            jax.ShapeDtypeStruct((3, QB, 8), F32),
        ],
        compiler_params=_cparams(),
        name="bias_tables",
    )(rel_bias, jnp.asarray(bk_near), jnp.asarray(bk_cmp), jnp.asarray(bk_dec), jnp.asarray(bk_decc),
      jnp.asarray(bk_dil))


def _compress_rows(rows_ref, pe_ref, w_ref, out_ref, nblk):
    for c in range(2):
        acc = jnp.zeros((2 * nblk, HD), F32)
        for l in range(CMP_BLOCK):
            pe = pe_ref[c, l:l + 1, :]
            xs = [rows_ref[pl.ds(4 * l + 2 * g + c, nblk, stride=4 * CMP_BLOCK), :] + pe for g in range(2)]
            acc = acc + _dot(jnp.concatenate(xs, axis=0).astype(BF16), w_ref[c, l])
        for g in range(2):
            out_ref[:, (2 * g + c) * HD:(2 * g + c + 1) * HD] = acc[g * nblk:(g + 1) * nblk]


CMP_ROWS_P = 4096


def _compress_prompt_kernel(rows_ref, pe_ref, w_ref, out_ref):
    _compress_rows(rows_ref, pe_ref, w_ref, out_ref, CMP_ROWS_P // CMP_BLOCK)


def _compress_prompt(z, pe, w):
    nblk = CMP_ROWS_P // CMP_BLOCK
    rows = z[:N_TOK_P, E_CMP:E_SEL].reshape(4 * N_TOK_P, HD)
    return pl.pallas_call(
        _compress_prompt_kernel,
        grid=(N_TOK_P // CMP_ROWS_P,),
        in_specs=[
            pl.BlockSpec((4 * CMP_ROWS_P, HD), lambda i: (i, 0)),
            pl.BlockSpec((2, CMP_BLOCK, HD), lambda i: (0, 0, 0)),
            pl.BlockSpec((2, CMP_BLOCK, HD, HD), lambda i: (0, 0, 0, 0)),
        ],
        out_specs=pl.BlockSpec((nblk, 512), lambda i: (i, 0)),
        out_shape=jax.ShapeDtypeStruct((N_TOK_P // CMP_BLOCK, 512), F32),
        compiler_params=_cparams(("parallel",)),
        name="nsa_compress_prompt",
    )(rows, pe, w)


CMP_SEQS = 2


def _compress_sample_kernel(pt_ref, *refs):
    pages = refs[:CMP_SEQS * N_PAGES]
    pe_ref, w_ref, out_ref, rows_scr = refs[CMP_SEQS * N_PAGES:]
    for k, pg in enumerate(pages):
        rows_scr[4 * k * PAGE:4 * (k + 1) * PAGE, :] = pg[...]
    _compress_rows(rows_scr, pe_ref, w_ref, out_ref, CMP_SEQS * N_CMP)


def _page_map(s_local, k, n_seqs, i, pt_ref):
    return (pt_ref[i * n_seqs + s_local, k], 0, 0)


def _compress_sample(pool, page_table, pe, w):
    page_specs = [pl.BlockSpec((None, 4 * PAGE, HD), functools.partial(_page_map, s, k, CMP_SEQS))
                  for s in range(CMP_SEQS) for k in range(N_PAGES)]
    nblk = CMP_SEQS * N_CMP
    grid_spec = pltpu.PrefetchScalarGridSpec(
        num_scalar_prefetch=1,
        grid=(DEC_BATCH // CMP_SEQS,),
        in_specs=page_specs + [
            pl.BlockSpec((2, CMP_BLOCK, HD), lambda i, pt: (0, 0, 0)),
            pl.BlockSpec((2, CMP_BLOCK, HD, HD), lambda i, pt: (0, 0, 0, 0)),
        ],
        out_specs=pl.BlockSpec((nblk, 512), lambda i, pt: (i, 0)),
        scratch_shapes=[pltpu.VMEM((4 * CMP_SEQS * PAST_LEN, HD), F32)],
    )
    return pl.pallas_call(
        _compress_sample_kernel,
        grid_spec=grid_spec,
        out_shape=jax.ShapeDtypeStruct((DEC_BATCH * N_CMP, 512), F32),
        compiler_params=_cparams(("parallel",)),
        name="nsa_compress_sample",
    )(page_table, *([pool] * (CMP_SEQS * N_PAGES)), pe, w)


def _even_first(cmp_sum, n_seq):
    x = cmp_sum.reshape(n_seq, N_CMP // 2, 2, 512)
    return jnp.swapaxes(x, 1, 2).reshape(n_seq, N_CMP, 512)


def _attend(q_bf, k_at, v_at, kb_lo, kb_hi, logit_fn, s_scr, weight_fn=None):
    rows = q_bf.shape[0]

    def pass1(kb, m):
        s = logit_fn(kb, _dot_nt(q_bf, k_at(kb)))
        s_scr[kb] = s
        return jnp.maximum(m, jnp.max(s, axis=-1, keepdims=True))

    m = lax.fori_loop(kb_lo, kb_hi, pass1, jnp.full((rows, 1), NEG_INF, F32))

    def pass2(kb, carry):
        l, acc = carry
        p = jnp.exp(s_scr[kb] - m)
        if weight_fn is not None:
            p = p * weight_fn(kb)
        return (l + jnp.sum(p, axis=-1, keepdims=True),
                acc + _dot(p.astype(BF16), v_at(kb)))

    l, acc = lax.fori_loop(kb_lo, kb_hi, pass2, (jnp.zeros((rows, 1), F32), jnp.zeros((rows, HD), F32)))
    return acc / jnp.maximum(l, 1e-30)


def _chunk(ref, kb, col):
    return ref[pl.ds(pl.multiple_of(kb * QB, QB), QB), col:col + HD]


def _mask_rows4(ok, logits):
    return jnp.where(ok[None], logits.reshape(4, QB, QB), NEG_INF).reshape(4 * QB, QB)


def _stack_heads(q_ref, g):
    return jnp.concatenate([q_ref[:, (4 * g + a) * HD:(4 * g + a + 1) * HD] for a in range(4)], axis=0)


def _rel_dist(i, kb):
    r = lax.broadcasted_iota(jnp.int32, (QB, QB), 0)
    c = lax.broadcasted_iota(jnp.int32, (QB, QB), 1)
    return (i - kb) * QB + r - c


def _rank_select(score, n_blocks, n_take):
    blk = lax.broadcasted_iota(jnp.int32, score.shape, 1)
    rank = jnp.zeros(score.shape, F32)
    for j in range(n_blocks):
        col = score[:, j:j + 1]
        beats = (col > score) | ((col == score) & (blk > j))
        rank = rank + jnp.where(beats, 1.0, 0.0)
    return jnp.where(rank < n_take, 1.0, 0.0)


def _nsa_prompt_kernel(q_ref, sm_ref, ksel_ref, kwin_ref, cmp_ref, bnear_ref, bcmp_ref, o_ref, s_scr):
    i = pl.program_id(1)
    sm = sm_ref[...]
    row = lax.broadcasted_iota(jnp.int32, (4 * QB, N_CMP), 0) & (QB - 1)
    col = lax.broadcasted_iota(jnp.int32, (4 * QB, N_CMP), 1)
    dist_c = i * QB + row - (CMP_BLOCK * _cmp_block_of_col(col) + CMP_BLOCK - 1)
    ok_c = dist_c >= 0
    t_q = i * QB + lax.broadcasted_iota(jnp.int32, (QB, N_CMP // 2), 0)
    blk = lax.broadcasted_iota(jnp.int32, (QB, N_CMP // 2), 1)
    cur = t_q // SEL_BLOCK
    forced = (blk == 0) | (blk == cur) | (blk == cur - 1)
    admissible = blk * SEL_BLOCK <= t_q
    for g in range(2):
        q = _stack_heads(q_ref, g) * Q_SCALE
        q_bf = q.astype(BF16)
        bias = lambda kb: bnear_ref[jnp.minimum(i - kb, 2), 4 * g:4 * g + 4].reshape(4 * QB, QB)
        logit = _dot_nt(q, cmp_ref[:, 2 * g * HD:(2 * g + 1) * HD], HI) + bcmp_ref[4 * g:4 * g + 4].reshape(4 * QB, N_CMP)
        logit = jnp.where(ok_c, logit, NEG_INF)
        e = jnp.where(ok_c, jnp.exp(logit - jnp.max(logit, axis=-1, keepdims=True)), 0.0)
        p_c = e / jnp.maximum(jnp.sum(e, axis=-1, keepdims=True), 1e-30)
        o_c = _dot(p_c.astype(BF16), cmp_ref[:, (2 * g + 1) * HD:(2 * g + 2) * HD].astype(BF16))
        p_g = p_c[0:QB] + p_c[QB:2 * QB] + p_c[2 * QB:3 * QB] + p_c[3 * QB:4 * QB]
        score = p_g[:, :N_CMP // 2] + p_g[:, N_CMP // 2:]
        score = jnp.where(forced, FORCE_SCORE, score)
        score = jnp.where(admissible, score, NEG_INF)
        sel = _rank_select(score, SEQ // SEL_BLOCK, N_SEL).astype(BF16)

        def sel_logits(kb, raw):
            jj = lax.broadcasted_iota(jnp.int32, (N_CMP // 2, QB), 0)
            cc = lax.broadcasted_iota(jnp.int32, (N_CMP // 2, QB), 1)
            expand = jnp.where(jj == 2 * kb + cc // SEL_BLOCK, 1.0, 0.0).astype(BF16)
            ok = (_dot(sel, expand) > 0.5) & (_rel_dist(i, kb) >= 0)
            return _mask_rows4(ok, raw + bias(kb))

        o_s = _attend(q_bf, lambda kb: _chunk(ksel_ref, kb, 2 * g * HD),
                      lambda kb: _chunk(ksel_ref, kb, (2 * g + 1) * HD), 0, i + 1, sel_logits, s_scr)

        def win_logits(kb, raw):
            d = _rel_dist(i, kb)
            return _mask_rows4((d >= 0) & (d <= A_WINDOW), raw + bias(kb))

        o_w = _attend(q_bf, lambda kb: _chunk(kwin_ref, kb, 2 * g * HD),
                      lambda kb: _chunk(kwin_ref, kb, (2 * g + 1) * HD),
                      jnp.maximum(i - A_WINDOW // QB, 0), i + 1, win_logits, s_scr)
        gates = []
        for br in range(3):
            gates.append(jnp.concatenate(
                [_sigmoid(sm[:, SM_GA + 3 * (4 * g + a) + br:SM_GA + 3 * (4 * g + a) + br + 1]) for a in range(4)],
                axis=0))
        out = gates[0] * o_c + gates[1] * o_s + gates[2] * o_w
        for a in range(4):
            o_ref[:, (4 * g + a) * HD:(4 * g + a + 1) * HD] = out[a * QB:(a + 1) * QB]


def _nsa_prompt(z, zb, cmp_sum, bnear, bcmp):
    return pl.pallas_call(
        _nsa_prompt_kernel,
        grid=(BATCH, NQB),
        in_specs=[
            pl.BlockSpec((QB, 1024), lambda b, i: (b * NQB + i, E_QA // 1024)),
            pl.BlockSpec((QB, 128), lambda b, i: (b * NQB + i, E_SM // 128)),
            pl.BlockSpec((SEQ, 512), lambda b, i: (b, E_SEL // 512)),
            pl.BlockSpec((SEQ, 512), lambda b, i: (b, E_WIN // 512)),
            pl.BlockSpec((None, N_CMP, 512), lambda b, i: (b, 0, 0)),
            pl.BlockSpec((3, 8, QB, QB), lambda b, i: (0, 0, 0, 0)),
            pl.BlockSpec((8, QB, N_CMP), lambda b, i: (0, i, 0)),
        ],
        out_specs=pl.BlockSpec((QB, 1024), lambda b, i: (b * NQB + i, 0)),
        out_shape=jax.ShapeDtypeStruct((N_TOK_P, 1024), F32),
        scratch_shapes=[pltpu.VMEM((NQB, 4 * QB, QB), F32)],
        compiler_params=_cparams(("parallel", "arbitrary")),
        name="nsa_prompt",
    )(z, z, zb, zb, cmp_sum, bnear, bcmp)


def _dsa_prompt_kernel(q_ref, qi_ref, sm_ref, kidx_ref, kv_ref, bnear_ref, o_ref, s_scr, sc_scr, mk_scr):
    i = pl.program_id(1)
    nkb = i + 1
    sm = sm_ref[...]
    qi = qi_ref[...]
    wi = sm[:, SM_WI:SM_WI + 4] * 0.5

    def score_body(kb, carry):
        lo, hi = carry
        kid = kidx_ref[pl.ds(pl.multiple_of(kb * QB, QB), QB), SM_KI:SM_KI + 64]
        sc = jnp.zeros((QB, QB), F32)
        for h in range(4):
            sc = sc + wi[:, h:h + 1] * jnp.maximum(_dot_nt(qi[:, 64 * h:64 * h + 64], kid, HI), 0.0)
        causal = _rel_dist(i, kb) >= 0
        sc_scr[kb] = jnp.where(causal, sc, NEG_INF)
        lo = jnp.minimum(lo, jnp.min(jnp.where(causal, sc, BIG), axis=-1, keepdims=True))
        hi = jnp.maximum(hi, jnp.max(jnp.where(causal, sc, -BIG), axis=-1, keepdims=True))
        return lo, hi

    lo, hib = lax.fori_loop(0, nkb, score_body, (jnp.full((QB, 1), BIG, F32), jnp.full((QB, 1), -BIG, F32)))
    n_causal = (i * QB + 1 + lax.broadcasted_iota(jnp.int32, (QB, 1), 0)).astype(F32)
    k_eff = jnp.minimum(n_causal, float(IDX_TOPK))

    def count(thr, strict):
        def body(kb, acc):
            s = sc_scr[kb]
            return acc + jnp.where((s > thr) if strict else (s >= thr), 1.0, 0.0)
        return jnp.sum(lax.fori_loop(0, nkb, body, jnp.zeros((QB, QB), F32)), axis=-1, keepdims=True)

    def max_below(bound):
        def body(kb, acc):
            s = sc_scr[kb]
            return jnp.maximum(acc, jnp.where(s < bound, s, -BIG))
        return jnp.max(lax.fori_loop(0, nkb, body, jnp.full((QB, QB), -BIG, F32)), axis=-1, keepdims=True)

    hix = jnp.full((QB, 1), BIG, F32)
    for _ in range(BISECT_STEPS):
        mid = 0.5 * (lo + hib)
        ge = count(mid, False) >= k_eff
        lo = jnp.where(ge, mid, lo)
        hib = jnp.where(ge, hib, mid)
        hix = jnp.where(ge, hix, mid)

    def step_cond(c):
        return c[3] > 0.5

    def step_body(c):
        bound, thr, done, _ = c
        cand = max_below(bound)
        ok = count(cand, False) >= k_eff
        active = done < 0.5
        thr = jnp.where(active & ok, cand, thr)
        bound = jnp.where(active & jnp.logical_not(ok), cand, bound)
        done = jnp.where(ok, 1.0, done)
        return bound, thr, done, jnp.sum(1.0 - done)

    _, thr, _, _ = lax.while_loop(step_cond, step_body,
                                  (hix, lo, jnp.zeros((QB, 1), F32), jnp.float32(QB)))
    need = k_eff - count(thr, True)
    upper = jnp.where(lax.broadcasted_iota(jnp.int32, (QB, QB), 0) <= lax.broadcasted_iota(jnp.int32, (QB, QB), 1),
                      1.0, 0.0).astype(BF16)

    def mask_body(kb, carry):
        s = sc_scr[kb]
        eq = s == thr
        eqf = jnp.where(eq, 1.0, 0.0)
        prefix = _dot(eqf.astype(BF16), upper) + carry
        keep = (s > thr) | (eq & (prefix <= need))
        mk_scr[kb] = jnp.where(keep & (_rel_dist(i, kb) >= 0), 1.0, 0.0)
        return carry + jnp.sum(eqf, axis=-1, keepdims=True)

    lax.fori_loop(0, nkb, mask_body, jnp.zeros((QB, 1), F32))

    for g in range(2):
        q_bf = (_stack_heads(q_ref, g) * Q_SCALE).astype(BF16)

        def logits(kb, raw):
            bias = bnear_ref[jnp.minimum(i - kb, 2), 4 * g:4 * g + 4].reshape(4 * QB, QB)
            return _mask_rows4(mk_scr[kb] > 0.5, raw + bias)

        out = _attend(q_bf, lambda kb: _chunk(kv_ref, kb, 2 * g * HD),
                      lambda kb: _chunk(kv_ref, kb, (2 * g + 1) * HD), 0, nkb, logits, s_scr)
        for a in range(4):
            o_ref[:, (4 * g + a) * HD:(4 * g + a + 1) * HD] = out[a * QB:(a + 1) * QB]


def _dsa_prompt(z, zb, bnear):
    return pl.pallas_call(
        _dsa_prompt_kernel,
        grid=(BATCH, NQB),
        in_specs=[
            pl.BlockSpec((QB, 1024), lambda b, i: (b * NQB + i, E_QB // 1024)),
            pl.BlockSpec((QB, 256), lambda b, i: (b * NQB + i, E_QI // 256)),
            pl.BlockSpec((QB, 128), lambda b, i: (b * NQB + i, E_SM // 128)),
            pl.BlockSpec((SEQ, 128), lambda b, i: (b, E_SM // 128)),
            pl.BlockSpec((SEQ, 512), lambda b, i: (b, E_KVB // 512)),
            pl.BlockSpec((3, 8, QB, QB), lambda b, i: (0, 0, 0, 0)),
        ],
        out_specs=pl.BlockSpec((QB, 1024), lambda b, i: (b * NQB + i, 0)),
        out_shape=jax.ShapeDtypeStruct((N_TOK_P, 1024), F32),
        scratch_shapes=[pltpu.VMEM((NQB, 4 * QB, QB), F32), pltpu.VMEM((NQB, QB, QB), F32),
                        pltpu.VMEM((NQB, QB, QB), F32)],
        compiler_params=_cparams(("parallel", "arbitrary")),
        name="dsa_prompt",
    )(z, z, z, z, zb, bnear)


DIL_CFG = ((128, 1), (512, 4), (2048, 16))
DIL_REL = 5


def _dil_count(dist):
    cnt = jnp.zeros(dist.shape, F32)
    for window, dil in DIL_CFG:
        cnt = cnt + jnp.where((dist >= 0) & (dist <= window) & ((dist & (dil - 1)) == 0), 1.0, 0.0)
    return cnt


def _dil_prompt_kernel(q_ref, kva_ref, kvb_ref, bnear_ref, o_ref, s_scr, cnt_scr):
    i = pl.program_id(1)
    for rel in range(DIL_REL + 1):
        cnt_scr[rel] = _dil_count(_rel_dist(rel, 0))
    for h in range(8):
        kv_ref = kva_ref if h < 4 else kvb_ref
        col = (h % 4) * 2 * HD
        q_bf = (q_ref[:, h * HD:(h + 1) * HD] * Q_SCALE).astype(BF16)
        weight = lambda kb: cnt_scr[jnp.minimum(i - kb, DIL_REL)]

        def logits(kb, raw):
            return jnp.where(weight(kb) > 0.5, raw + bnear_ref[jnp.minimum(i - kb, 2), h], NEG_INF)

        o_ref[:, h * HD:(h + 1) * HD] = _attend(
            q_bf, lambda kb: _chunk(kv_ref, kb, col), lambda kb: _chunk(kv_ref, kb, col + HD),
            0, i + 1, logits, s_scr, weight)


def _dil_prompt(z, zb, bnear):
    return pl.pallas_call(
        _dil_prompt_kernel,
        grid=(BATCH, NQB),
        in_specs=[
            pl.BlockSpec((QB, 1024), lambda b, i: (b * NQB + i, O_QC // 1024)),
            pl.BlockSpec((SEQ, 1024), lambda b, i: (b, O_KVC // 1024)),
            pl.BlockSpec((SEQ, 1024), lambda b, i: (b, O_KVC // 1024 + 1)),
            pl.BlockSpec((3, 8, QB, QB), lambda b, i: (0, 0, 0, 0)),
        ],
        out_specs=pl.BlockSpec((QB, 1024), lambda b, i: (b * NQB + i, 0)),
        out_shape=jax.ShapeDtypeStruct((N_TOK_P, 1024), F32),
        scratch_shapes=[pltpu.VMEM((NQB, QB, QB), F32), pltpu.VMEM((DIL_REL + 1, QB, QB), F32)],
        compiler_params=_cparams(("parallel", "arbitrary")),
        name="dilated_prompt",
    )(z, zb, zb, bnear)


def _rope_rows(x, cos2, sin2):
    return x * cos2 + pltpu.roll(x, 64, 1) * sin2


def _readout(o, gate, gn):
    mu = jnp.mean(o, axis=-1, keepdims=True)
    d = o - mu
    var = jnp.mean(d * d, axis=-1, keepdims=True)
    return gate * _sigmoid(gate) * (d * lax.rsqrt(var + RMS_EPS)) * gn


def _ret_prompt_kernel(q_ref, k_ref, v_ref, gr_ref, cos_ref, sin_ref, gn_ref, o_ref, st_ref, st_scr):
    ci = pl.program_id(1)

    @pl.when(ci == 0)
    def _():
        st_scr[...] = jnp.zeros_like(st_scr)

    cos2, sin2 = cos_ref[...], sin_ref[...]
    r = lax.broadcasted_iota(jnp.int32, (QB, QB), 0)
    c = lax.broadcasted_iota(jnp.int32, (QB, QB), 1)
    rel = (r - c).astype(F32)
    pos_col = (lax.broadcasted_iota(jnp.int32, (QB, 1), 0)).astype(F32)
    for h in range(R_HEADS):
        lg = LOG_G[h]
        q = _rope_rows(q_ref[:, h * R_DK:(h + 1) * R_DK], cos2, sin2)
        k = _rope_rows(k_ref[:, h * R_DK:(h + 1) * R_DK], cos2, sin2) * RK_SCALE
        v = v_ref[:, h * R_DV:(h + 1) * R_DV]
        state = st_scr[h]
        decay = jnp.where(r >= c, jnp.exp(lg * jnp.maximum(rel, 0.0)), 0.0)
        q_bf, v_bf = q.astype(BF16), v.astype(BF16)
        scores = _dot_nt(q_bf, k.astype(BF16)) * decay
        out = _dot(scores.astype(BF16), v_bf)
        out = out + _dot(q_bf, state.astype(BF16)) * jnp.exp(lg * (pos_col + 1.0))
        zeta = jnp.exp(lg * (QB - 1.0 - pos_col))
        new_state = state * math.exp(lg * QB) + _dot((k * zeta).T.astype(BF16), v_bf)
        st_scr[h] = new_state
        st_ref[h] = new_state
        o_ref[:, h * R_DV:(h + 1) * R_DV] = _readout(out, gr_ref[:, h * R_DV:(h + 1) * R_DV],
                                                     gn_ref[:, h * R_DV:(h + 1) * R_DV])


def _ret_prompt(z, cos2, sin2, gn):
    return pl.pallas_call(
        _ret_prompt_kernel,
        grid=(BATCH, NQB),
        in_specs=[
            pl.BlockSpec((QB, 512), lambda b, i: (b * NQB + i, O_QR // 512)),
            pl.BlockSpec((QB, 512), lambda b, i: (b * NQB + i, O_KR // 512)),
            pl.BlockSpec((QB, 1024), lambda b, i: (b * NQB + i, O_VR // 1024)),
            pl.BlockSpec((QB, 1024), lambda b, i: (b * NQB + i, O_GR // 1024)),
            pl.BlockSpec((QB, 128), lambda b, i: (i, 0)),
            pl.BlockSpec((QB, 128), lambda b, i: (i, 0)),
            pl.BlockSpec((1, 1024), lambda b, i: (0, 0)),
        ],
        out_specs=[pl.BlockSpec((QB, 1024), lambda b, i: (b * NQB + i, 0)),
                   pl.BlockSpec((None, R_HEADS, R_DK, R_DV), lambda b, i: (b, 0, 0, 0))],
        out_shape=[jax.ShapeDtypeStruct((N_TOK_P, 1024), F32),
                   jax.ShapeDtypeStruct((BATCH, R_HEADS, R_DK, R_DV), F32)],
        scratch_shapes=[pltpu.VMEM((R_HEADS, R_DK, R_DV), F32)],
        compiler_params=_cparams(("parallel", "arbitrary")),
        name="retention_prompt",
    )(z, z, z, z, cos2, sin2, gn.reshape(1, 1024))


DI_QA, DI_QB, DI_KSEL, DI_VSEL, DI_KWIN, DI_VWIN, DI_KB, DI_VB, DI_QI, DI_KI, DI_MISC = (
    0, 128, 256, 384, 512, 640, 768, 896, 1024, 1088, 1152)
DEC_IN_W = 1280
DEC_SEL_BLOCKS = PAST_LEN // SEL_BLOCK + 1


def _decode_attend(q, q_bf, chunks, k_new, v_new, bias_new, new_ok, logit_fn):
    row = lax.broadcasted_iota(jnp.int32, (8, QB), 0)
    pick = lambda a, b: jnp.where(row < 4, a, b)
    logits = []
    for k, kv in enumerate(chunks):
        kv_bf = kv.astype(BF16)
        raw = pick(_dot_nt(q_bf, kv_bf[:, 0:HD]), _dot_nt(q_bf, kv_bf[:, 2 * HD:3 * HD]))
        logits.append(logit_fn(k, raw))
    s_new = jnp.sum(q * k_new, axis=-1, keepdims=True) + bias_new
    if new_ok is not None:
        s_new = jnp.where(new_ok, s_new, NEG_INF)
    m = s_new
    for s in logits:
        m = jnp.maximum(m, jnp.max(s, axis=-1, keepdims=True))
    p_new = jnp.exp(s_new - m)
    l = p_new
    acc = p_new * v_new
    for s, kv in zip(logits, chunks):
        p = jnp.exp(s - m)
        l = l + jnp.sum(p, axis=-1, keepdims=True)
        p_bf = p.astype(BF16)
        kv_bf = kv.astype(BF16)
        acc = acc + pick(_dot(p_bf, kv_bf[:, HD:2 * HD]), _dot(p_bf, kv_bf[:, 3 * HD:4 * HD]))
    return acc / jnp.maximum(l, 1e-30)


def _even_sample_kernel(pt_ref, *refs):
    sel_pages = refs[0:N_PAGES]
    kvb_pages = refs[N_PAGES:2 * N_PAGES]
    idx_pages = refs[2 * N_PAGES:3 * N_PAGES]
    din_ref, cmp_ref, win_ref, expand_ref, dec_ref, decc_ref, o_ref = refs[3 * N_PAGES:]
    d = din_ref[...]
    row = lax.broadcasted_iota(jnp.int32, (8, QB), 0)
    row_c = lax.broadcasted_iota(jnp.int32, (8, N_CMP), 0)
    pick_c = lambda a, b: jnp.where(row_c < 4, a, b)
    pick = lambda a, b: jnp.where(row < 4, a, b)
    dec_a, dec_b = dec_ref[0:8, :], dec_ref[8:16, :]

    qa = d[:, DI_QA:DI_QA + HD] * Q_SCALE
    qa_bf = qa.astype(BF16)
    cmp = cmp_ref[...]
    logit = pick_c(_dot_nt(qa, cmp[:, 0:HD], HI), _dot_nt(qa, cmp[:, 2 * HD:3 * HD], HI)) + decc_ref[...]
    e = jnp.exp(logit - jnp.max(logit, axis=-1, keepdims=True))
    p_c = e / jnp.maximum(jnp.sum(e, axis=-1, keepdims=True), 1e-30)
    p_bf = p_c.astype(BF16)
    cmp_bf = cmp.astype(BF16)
    o_c = pick(_dot(p_bf, cmp_bf[:, HD:2 * HD]), _dot(p_bf, cmp_bf[:, 3 * HD:4 * HD]))
    g0 = jnp.sum(jnp.where(row_c < 4, p_c, 0.0), axis=0, keepdims=True)
    g1 = jnp.sum(jnp.where(row_c < 4, 0.0, p_c), axis=0, keepdims=True)
    p_g = pick_c(jnp.broadcast_to(g0, (8, N_CMP)), jnp.broadcast_to(g1, (8, N_CMP)))
    blk = lax.broadcasted_iota(jnp.int32, (8, N_CMP), 1)
    score = jnp.concatenate([p_g[:, :N_CMP // 2] + p_g[:, N_CMP // 2:], jnp.zeros((8, N_CMP // 2), F32)], axis=1)
    cur = PAST_LEN // SEL_BLOCK
    score = jnp.where((blk == 0) | (blk == cur) | (blk == cur - 1), FORCE_SCORE, score)
    score = jnp.where(blk < DEC_SEL_BLOCKS, score, -BIG)
    sel = _rank_select(score, DEC_SEL_BLOCKS, N_SEL) * jnp.where(blk < DEC_SEL_BLOCKS, 1.0, 0.0)
    sel_pos = _dot(sel.astype(BF16), expand_ref[...])

    def sel_logits(k, raw):
        return jnp.where(sel_pos[:, k * PAGE:(k + 1) * PAGE] > 0.5, raw + dec_a[:, k * PAGE:(k + 1) * PAGE], NEG_INF)

    o_s = _decode_attend(qa, qa_bf, [pg[...] for pg in sel_pages], d[:, DI_KSEL:DI_KSEL + HD],
                         d[:, DI_VSEL:DI_VSEL + HD], dec_a[:, PAST_LEN:PAST_LEN + 1], None, sel_logits)
    win0 = PAST_LEN - A_WINDOW

    def win_logits(k, raw):
        return raw + dec_a[:, win0 + k * QB:win0 + (k + 1) * QB]

    o_w = _decode_attend(qa, qa_bf, [win_ref[k * QB:(k + 1) * QB, :] for k in range(A_WINDOW // QB)],
                         d[:, DI_KWIN:DI_KWIN + HD], d[:, DI_VWIN:DI_VWIN + HD],
                         dec_a[:, PAST_LEN:PAST_LEN + 1], None, win_logits)
    gate = _sigmoid(d[:, DI_MISC:DI_MISC + 3])
    o_ref[:, 0:HD] = gate[:, 0:1] * o_c + gate[:, 1:2] * o_s + gate[:, 2:3] * o_w

    qi = d[:, DI_QI:DI_QI + 64]
    wi = d[:, DI_MISC + 3:DI_MISC + 4] * 0.5
    row16 = lax.broadcasted_iota(jnp.int32, (N_PAGES, PAGE), 0)
    sc = jnp.zeros((N_PAGES, PAGE), F32)
    for k, pg in enumerate(idx_pages):
        sc_k = jnp.sum(wi * jnp.maximum(_dot_nt(qi, pg[...], HI), 0.0), axis=0, keepdims=True)
        sc = jnp.where(row16 == k, sc_k, sc)
    sc_new = jnp.sum(wi * jnp.maximum(jnp.sum(qi * d[:, DI_KI:DI_KI + 64], axis=-1, keepdims=True), 0.0),
                     axis=0, keepdims=True)
    total = lambda x: jnp.sum(jnp.sum(x, axis=-1, keepdims=True), axis=0, keepdims=True)

    def count(thr, strict):
        hit = (lambda x: x > thr) if strict else (lambda x: x >= thr)
        return total(jnp.where(hit(sc), 1.0, 0.0)) + jnp.where(hit(sc_new), 1.0, 0.0)

    def max_below(bound):
        best = jnp.max(jnp.max(jnp.where(sc < bound, sc, -BIG), axis=-1, keepdims=True), axis=0, keepdims=True)
        return jnp.maximum(best, jnp.where(sc_new < bound, sc_new, -BIG))

    k_top = float(IDX_TOPK)
    lo = jnp.minimum(jnp.min(jnp.min(sc, axis=-1, keepdims=True), axis=0, keepdims=True), sc_new)
    hib = jnp.maximum(jnp.max(jnp.max(sc, axis=-1, keepdims=True), axis=0, keepdims=True), sc_new)
    hix = jnp.full((1, 1), BIG, F32)
    for _ in range(BISECT_STEPS):
        mid = 0.5 * (lo + hib)
        ge = count(mid, False) >= k_top
        lo = jnp.where(ge, mid, lo)
        hib = jnp.where(ge, hib, mid)
        hix = jnp.where(ge, hix, mid)

    def step_body(c):
        bound, thr, _ = c
        cand = max_below(bound)
        ok = count(cand, False) >= k_top
        return jnp.where(ok, bound, cand), jnp.where(ok, cand, thr), jnp.sum(jnp.where(ok, 0.0, 1.0))

    _, thr, _ = lax.while_loop(lambda c: c[2] > 0.5, step_body, (hix, lo, jnp.float32(1.0)))
    n_gt = count(thr, True)
    need = k_top - n_gt
    eq = jnp.where(sc == thr, 1.0, 0.0).astype(BF16)
    r128 = lax.broadcasted_iota(jnp.int32, (QB, QB), 0)
    c128 = lax.broadcasted_iota(jnp.int32, (QB, QB), 1)
    upper = jnp.where(r128 <= c128, 1.0, 0.0).astype(BF16)
    r16 = lax.broadcasted_iota(jnp.int32, (N_PAGES, N_PAGES), 0)
    c16 = lax.broadcasted_iota(jnp.int32, (N_PAGES, N_PAGES), 1)
    lower = jnp.where(c16 < r16, 1.0, 0.0).astype(BF16)
    prefix = _dot(eq, upper) + jnp.sum(_dot(lower, eq), axis=-1, keepdims=True)
    keep = jnp.where((sc > thr) | ((sc == thr) & (prefix <= need)), 1.0, 0.0)
    n_eq_past = total(eq.astype(F32))
    new_ok = (sc_new > thr) | ((sc_new == thr) & (n_gt + n_eq_past + 1.0 <= k_top))

    qb = d[:, DI_QB:DI_QB + HD] * Q_SCALE

    def b_logits(k, raw):
        return jnp.where(keep[k:k + 1, :] > 0.5, raw + dec_b[:, k * PAGE:(k + 1) * PAGE], NEG_INF)

    o_ref[:, HD:2 * HD] = _decode_attend(
        qb, qb.astype(BF16), [pg[...] for pg in kvb_pages], d[:, DI_KB:DI_KB + HD], d[:, DI_VB:DI_VB + HD],
        dec_b[:, PAST_LEN:PAST_LEN + 1], new_ok, b_logits)


def _seq_page_map(k, s, pt_ref):
    return (pt_ref[s, k], 0, 0)


def _even_sample(page_table, pool_sel, pool_b, pool_idx, din, cmp_sum, win_buf, expand, dec, decc):
    specs = []
    for width in (512, 512, 64):
        specs += [pl.BlockSpec((None, PAGE, width), functools.partial(_seq_page_map, k)) for k in range(N_PAGES)]
    specs += [
        pl.BlockSpec((None, 8, DEC_IN_W), lambda s, pt: (s, 0, 0)),
        pl.BlockSpec((None, N_CMP, 512), lambda s, pt: (s, 0, 0)),
        pl.BlockSpec((None, A_WINDOW, 512), lambda s, pt: (s, 0, 0)),
        pl.BlockSpec((N_CMP, PAST_LEN), lambda s, pt: (0, 0)),
        pl.BlockSpec((16, DEC_W), lambda s, pt: (0, 0)),
        pl.BlockSpec((8, N_CMP), lambda s, pt: (0, 0)),
    ]
    grid_spec = pltpu.PrefetchScalarGridSpec(
        num_scalar_prefetch=1, grid=(DEC_BATCH,), in_specs=specs,
        out_specs=pl.BlockSpec((None, 8, 2 * HD), lambda s, pt: (s, 0, 0)))
    return pl.pallas_call(
        _even_sample_kernel,
        grid_spec=grid_spec,
        out_shape=jax.ShapeDtypeStruct((DEC_BATCH, 8, 2 * HD), F32),
        compiler_params=_cparams(("parallel",)),
        name="even_sample",
    )(page_table, *([pool_sel] * N_PAGES), *([pool_b] * N_PAGES), *([pool_idx] * N_PAGES),
      din, cmp_sum, win_buf, expand, dec, decc)


def _odd_sample_kernel(c0_ref, c1_ref, c2_ref, qkv_ref, rrow_ref, kcol_ref, st_ref, bdil_ref, tbl_ref,
                       rope_ref, ropec_ref, gn_ref, oc_ref, od_ref, st_out_ref):
    qkv = qkv_ref[...]
    q = qkv[:, 0:HD] * Q_SCALE
    lane = lax.broadcasted_iota(jnp.int32, (QB, 8), 1)
    lane1 = lax.broadcasted_iota(jnp.int32, (1, 8), 1)
    cfg_refs = (c0_ref, c1_ref, c2_ref)
    logits = []
    for cfg in range(3):
        s = bdil_ref[cfg]
        for h in range(8):
            s_h = jnp.sum(cfg_refs[cfg][:, 2 * h * HD:(2 * h + 1) * HD] * q[h:h + 1, :], axis=-1, keepdims=True)
            s = s + jnp.where(lane == h, s_h, 0.0)
        logits.append(s)
    s_new = tbl_ref[0:1, 0:8]
    for h in range(8):
        s_h = jnp.sum(q[h:h + 1, :] * qkv[h:h + 1, HD:2 * HD], axis=-1, keepdims=True)
        s_new = s_new + jnp.where(lane1 == h, s_h, 0.0)
    m = s_new
    for s in logits:
        m = jnp.maximum(m, jnp.max(s, axis=0, keepdims=True))
    p_new = 3.0 * jnp.exp(s_new - m)
    probs = [jnp.exp(s - m) for s in logits]
    l = p_new
    for p in probs:
        l = l + jnp.sum(p, axis=0, keepdims=True)
    inv = 1.0 / jnp.maximum(l, 1e-30)
    for h in range(8):
        acc = p_new[:, h:h + 1] * qkv[h:h + 1, 2 * HD:3 * HD]
        for cfg in range(3):
            v = cfg_refs[cfg][:, (2 * h + 1) * HD:(2 * h + 2) * HD]
            acc = acc + jnp.sum(probs[cfg][:, h:h + 1] * v, axis=0, keepdims=True)
        oc_ref[:, h * HD:(h + 1) * HD] = acc * inv[:, h:h + 1]

    rrow = rrow_ref[...]
    cos2, sin2 = rope_ref[0:1, :], rope_ref[1:2, :]
    qr = _rope_rows(rrow[:, 0:R_DK], cos2, sin2)
    kr = _rope_rows(rrow[:, R_DK:2 * R_DK], cos2, sin2) * RK_SCALE
    qk = jnp.sum(qr * kr, axis=-1, keepdims=True)
    kc = kcol_ref[...]
    kc = (kc * ropec_ref[:, 0:1] + jnp.concatenate([kc[64:], kc[:64]], axis=0) * ropec_ref[:, 1:2]) * RK_SCALE
    qr_bf = jnp.concatenate([qr, jnp.zeros_like(qr)], axis=0).astype(BF16)
    row8 = lax.broadcasted_iota(jnp.int32, (8, R_DV), 0)
    for h in range(R_HEADS):
        gamma = math.exp(LOG_G[h])
        state = st_ref[h]
        v = rrow[h:h + 1, 2 * R_DK:2 * R_DK + R_DV]
        q_state = jnp.sum(jnp.where(row8 == h, _dot(qr_bf, state.astype(BF16)), 0.0), axis=0, keepdims=True)
        out = qk[h:h + 1, :] * v + q_state * gamma
        st_out_ref[h] = state * gamma + kc[:, h:h + 1] * v
        od_ref[:, h * R_DV:(h + 1) * R_DV] = _readout(out, rrow[h:h + 1, 2 * R_DK + R_DV:2 * R_DK + 2 * R_DV],
                                                      gn_ref[:, h * R_DV:(h + 1) * R_DV])


def _odd_sample(c_buf, qkv, rrow, kcol, state, bdil, rel_bias, rope_row, rope_col, gn):
    cw = 8 * 2 * HD
    views = [c_buf.reshape(DEC_BATCH, PAST_LEN // dil, dil * cw) for _, dil in DIL_CFG]
    c_specs = [pl.BlockSpec((None, QB, cw), functools.partial(lambda nb, s: (s, nb, 0), PAST_LEN // dil // QB - 1))
               for _, dil in DIL_CFG]
    return pl.pallas_call(
        _odd_sample_kernel,
        grid=(DEC_BATCH,),
        in_specs=c_specs + [
            pl.BlockSpec((None, 8, 3 * HD), lambda s: (s, 0, 0)),
            pl.BlockSpec((None, R_HEADS, 2 * R_DK + 2 * R_DV), lambda s: (s, 0, 0)),
            pl.BlockSpec((None, R_DK, R_HEADS), lambda s: (s, 0, 0)),
            pl.BlockSpec((None, R_HEADS, R_DK, R_DV), lambda s: (s, 0, 0, 0)),
            pl.BlockSpec((3, QB, 8), lambda s: (0, 0, 0)),
            pl.BlockSpec((N_BUCKETS, 16), lambda s: (0, 0)),
            pl.BlockSpec((2, R_DK), lambda s: (0, 0)),
            pl.BlockSpec((R_DK, 2), lambda s: (0, 0)),
            pl.BlockSpec((1, 1024), lambda s: (0, 0)),
        ],
        out_specs=[pl.BlockSpec((None, 1, 1024), lambda s: (s, 0, 0)),
                   pl.BlockSpec((None, 1, 1024), lambda s: (s, 0, 0)),
                   pl.BlockSpec((None, R_HEADS, R_DK, R_DV), lambda s: (s, 0, 0, 0))],
        out_shape=[jax.ShapeDtypeStruct((DEC_BATCH, 1, 1024), F32),
                   jax.ShapeDtypeStruct((DEC_BATCH, 1, 1024), F32),
                   jax.ShapeDtypeStruct((DEC_BATCH, R_HEADS, R_DK, R_DV), F32)],
        compiler_params=_cparams(("parallel",)),
        name="odd_sample",
    )(*views, qkv, rrow, kcol, state, bdil, rel_bias, rope_row, rope_col, gn.reshape(1, 1024))


def _rope_tables(pos):
    half = R_DK // 2
    inv = 10000.0 ** (-jnp.arange(half, dtype=F32) / half)
    ang = pos.astype(F32)[:, None] * inv[None, :]
    cos, sin = jnp.cos(ang), jnp.sin(ang)
    return jnp.concatenate([cos, cos], axis=-1), jnp.concatenate([-sin, sin], axis=-1)


def _permute_even_w(w):
    qa, kva, ga, qb, kvb, qi, ki, wi = jnp.split(w, [1024, 2560, 2584, 3608, 4120, 4376, 4440], axis=1)
    pad = jnp.zeros((w.shape[0], EVEN_W - w.shape[1]), w.dtype)
    return jnp.concatenate([qa, qb, kva, kvb, qi, ki, wi, ga, pad], axis=1)


def _even_decode(zs, e, bias, page_table, cache_a_cmp, cache_a_sel, cache_b_kv, cache_b_idx, state_a_win,
                 cmp_pe, cmp_w_bf):
    dec, decc = bias[2], bias[3]
    n_pool = cache_a_cmp.shape[1]
    cmp_s = _even_first(_compress_sample(cache_a_cmp[e].reshape(n_pool, 4 * PAGE, HD), page_table, cmp_pe, cmp_w_bf),
                        DEC_BATCH)
    heads = lambda a: a.reshape(DEC_BATCH, 8, HD)
    rep = lambda a: jnp.repeat(a.reshape(DEC_BATCH, 2, HD), 4, axis=1)
    kva = zs[:, E_CMP:E_KVB].reshape(DEC_BATCH, 3, 2, 2, HD)
    kvb = zs[:, E_KVB:E_QI].reshape(DEC_BATCH, 2, 2, HD)
    sm = zs[:, E_SM:E_SM + 128]
    zero4 = lambda a: jnp.concatenate([a, jnp.zeros_like(a)], axis=1)
    qi8 = zero4(zs[:, E_QI:E_SM].reshape(DEC_BATCH, 4, 64))
    ki8 = jnp.broadcast_to(sm[:, None, SM_KI:SM_KI + 64], (DEC_BATCH, 8, 64))
    gates = sm[:, SM_GA:SM_GA + 24].reshape(DEC_BATCH, 8, 3)
    wi8 = zero4(sm[:, SM_WI:SM_WI + 4].reshape(DEC_BATCH, 4, 1))
    misc = jnp.concatenate([gates, wi8, jnp.zeros((DEC_BATCH, 8, DEC_IN_W - DI_MISC - 4), F32)], axis=2)
    din = jnp.concatenate([
        heads(zs[:, E_QA:E_QB]), heads(zs[:, E_QB:E_CMP]),
        rep(kva[:, 1, :, 0]), rep(kva[:, 1, :, 1]), rep(kva[:, 2, :, 0]), rep(kva[:, 2, :, 1]),
        rep(kvb[:, :, 0]), rep(kvb[:, :, 1]), qi8, ki8, misc], axis=2)
    expand = (jnp.arange(PAST_LEN)[None, :] // SEL_BLOCK == jnp.arange(N_CMP)[:, None]).astype(BF16)
    o_s = _even_sample(page_table, cache_a_sel[e].reshape(n_pool, PAGE, 512),
                       cache_b_kv[e].reshape(n_pool, PAGE, 512), cache_b_idx[e],
                       din, cmp_s, state_a_win[e].reshape(DEC_BATCH, A_WINDOW, 512), expand, dec, decc)
    return o_s[:, :, :HD].reshape(DEC_BATCH, 1024), o_s[:, :, HD:].reshape(DEC_BATCH, 1024)


def _even_layer(x, e, bias, page_table, cache_a_cmp, cache_a_sel, cache_b_kv, cache_b_idx, state_a_win,
                g, w_in, w_out, cmp_pe, cmp_w):
    bnear, bcmp = bias[0], bias[1]
    z, zb = _proj_in(x, g, _permute_even_w(w_in).astype(BF16), 896)
    cmp_w_bf = cmp_w.astype(BF16)
    cmp_p = _even_first(_compress_prompt(z, cmp_pe, cmp_w_bf), BATCH)
    oa_p = _nsa_prompt(z, zb, cmp_p, bnear[:, 0:8], bcmp)
    ob_p = _dsa_prompt(z, zb, bnear[:, 8:16])
    zs = z[N_TOK_P:]
    oa_s, ob_s = _even_decode(zs, e, bias, page_table, cache_a_cmp, cache_a_sel, cache_b_kv, cache_b_idx,
                              state_a_win, cmp_pe, cmp_w_bf)
    oa = jnp.concatenate([oa_p, oa_s], axis=0)
    ob = jnp.concatenate([ob_p, ob_s], axis=0)
    x = _proj_out(x, oa, ob, w_out[:1024].astype(BF16), w_out[1024:].astype(BF16))

    zp = z[:N_TOK_P].reshape(BATCH, SEQ, EVEN_W)
    kv5 = lambda a, n: a.reshape(n, -1, 2, 2, HD)
    state = (
        kv5(zp[:, :, E_CMP:E_SEL], BATCH), kv5(zs[:, E_CMP:E_SEL], DEC_BATCH),
        kv5(zp[:, :, E_SEL:E_WIN], BATCH), kv5(zs[:, E_SEL:E_WIN], DEC_BATCH),
        kv5(zp[:, :, E_KVB:E_QI], BATCH), kv5(zs[:, E_KVB:E_QI], DEC_BATCH),
        zp[:, :, E_SM:E_SM + 64], zs[:, None, E_SM:E_SM + 64],
        kv5(zp[:, SEQ - A_WINDOW:, E_WIN:E_KVB], BATCH), kv5(zs[:, E_WIN:E_KVB], DEC_BATCH),
    )
    return x, state


def _odd_layer(x, o, bias, state_c_win, state_ret, g, w_in, w_out, gn):
    bnear, bdil = bias[0], bias[4]
    z, zb = _proj_in(x, g, w_in.astype(BF16), 1536)
    cos_p, sin_p = _rope_tables(jnp.arange(SEQ))
    oc_p = _dil_prompt(z, zb, bnear[:, 0:8])
    od_p, ret_p = _ret_prompt(z, cos_p, sin_p, gn)

    zs = z[N_TOK_P:]
    kvc = zs[:, O_KVC:O_QR].reshape(DEC_BATCH, 8, 2, HD)
    qkv = jnp.concatenate([zs[:, O_QC:O_KVC].reshape(DEC_BATCH, 8, HD), kvc[:, :, 0], kvc[:, :, 1]], axis=2)
    rrow = jnp.concatenate([zs[:, O_QR:O_KR].reshape(DEC_BATCH, R_HEADS, R_DK),
                            zs[:, O_KR:O_VR].reshape(DEC_BATCH, R_HEADS, R_DK),
                            zs[:, O_VR:O_GR].reshape(DEC_BATCH, R_HEADS, R_DV),
                            zs[:, O_GR:].reshape(DEC_BATCH, R_HEADS, R_DV)], axis=2)
    kcol = jnp.swapaxes(zs[:, O_KR:O_VR].reshape(DEC_BATCH, R_HEADS, R_DK), 1, 2)
    cos_s, sin_s = _rope_tables(jnp.full((1,), PAST_LEN))
    rope_row = jnp.concatenate([cos_s, sin_s], axis=0)
    oc_s, od_s, ret_s = _odd_sample(state_c_win[o].reshape(DEC_BATCH, PAST_LEN, 8 * 2 * HD), qkv, rrow, kcol,
                                    state_ret[o], bdil, bias[5], rope_row, rope_row.T, gn)
    oc = jnp.concatenate([oc_p, oc_s.reshape(DEC_BATCH, 1024)], axis=0)
    od = jnp.concatenate([od_p, od_s.reshape(DEC_BATCH, 1024)], axis=0)
    x = _proj_out(x, oc, od, w_out[:1024].astype(BF16), w_out[1024:].astype(BF16))
    state = (z[:N_TOK_P, O_KVC:O_QR].reshape(BATCH, SEQ, 8, 2, HD), kvc[:, None], ret_p, ret_s)
    return x, state


def kernel(x_prompt, x_sample, cache_a_cmp, cache_a_sel, cache_b_kv, cache_b_idx, state_a_win, state_c_win,
           state_ret, page_table, p_prompt, p_sample, rel_bias, norm_g, final_norm, ffn_w_in, ffn_w_out,
           ple_gate, ple_proj, even_w_in, even_w_out, nsa_cmp_pe, nsa_cmp_w, odd_w_in, odd_w_out, ret_gn):
    depth = norm_g.shape[0]
    x = jnp.concatenate([x_prompt.reshape(N_TOK_P, D_MODEL), x_sample.reshape(DEC_BATCH, D_MODEL)], axis=0)
    bias = tuple(_bias_tables(rel_bias)) + (rel_bias,)
    w_in_bf = ffn_w_in.astype(BF16)
    w_out_bf = ffn_w_out.astype(BF16)
    even_states, odd_states = [], []
    for i in range(depth):
        x = _ffn_half(x, norm_g[i, 0], w_in_bf[i, 0], w_out_bf[i, 0])
        if i % 2 == 0:
            e = i // 2
            x, st = _even_layer(x, e, bias, page_table, cache_a_cmp, cache_a_sel, cache_b_kv, cache_b_idx,
                                state_a_win, norm_g[i, 1], even_w_in[e], even_w_out[e], nsa_cmp_pe[e], nsa_cmp_w[e])
            even_states.append(st)
        else:
            o = i // 2
            x, st = _odd_layer(x, o, bias, state_c_win, state_ret, norm_g[i, 1], odd_w_in[o], odd_w_out[o],
                               ret_gn[o])
            odd_states.append(st)
        x = _ffn_half(x, norm_g[i, 2], w_in_bf[i, 1], w_out_bf[i, 1])
        p = jnp.concatenate([p_prompt[i].reshape(N_TOK_P, PLE_DIM), p_sample[i].reshape(DEC_BATCH, PLE_DIM)], axis=0)
        x = _ple(x, p, norm_g[i, 3], ple_gate[i].astype(BF16), ple_proj[i].astype(BF16), final_norm,
                 final=(i == depth - 1))
    y_prompt = x[:N_TOK_P].reshape(BATCH, SEQ, D_MODEL)
    y_sample = x[N_TOK_P:].reshape(DEC_BATCH, 1, D_MODEL)
    ev = [jnp.stack([st[k] for st in even_states]) for k in range(10)]
    od = [jnp.stack([st[k] for st in odd_states]) for k in range(4)]
    return (y_prompt, y_sample, *ev, *od)
```

```python
import functools
import math

import numpy as np
import jax
import jax.numpy as jnp
from jax import lax
from jax.experimental import pallas as pl
from jax.experimental.pallas import tpu as pltpu

F32 = jnp.float32
BF16 = jnp.bfloat16
HI = lax.Precision.HIGHEST

D_MODEL = 2048
BATCH = 4
SEQ = 2048
DEC_BATCH = 128
PAST_LEN = 2048
PAGE = 128
N_PAGES = PAST_LEN // PAGE
HD = 128
D_FF = 5632
PLE_DIM = 256
N_TOK_P = BATCH * SEQ
N_TOK = N_TOK_P + DEC_BATCH
QB = 128
NQB = SEQ // QB
CMP_BLOCK = 32
N_CMP = SEQ // CMP_BLOCK
SEL_BLOCK = 64
N_SEL = 16
A_WINDOW = 512
IDX_TOPK = 256
R_HEADS = 4
R_DK = 128
R_DV = 256
N_BUCKETS = 32
RMS_EPS = 1e-6
NEG_INF = -1e30
BIG = 3.0e38
FORCE_SCORE = 1e4
Q_SCALE = HD ** -0.5
RK_SCALE = R_DK ** -0.5
EVEN_W = 4480
ODD_W = 6144
VMEM_LIMIT = 56 * 1024 * 1024
BISECT_STEPS = 10

E_QA, E_QB, E_CMP, E_SEL, E_WIN, E_KVB, E_QI, E_SM = 0, 1024, 2048, 2560, 3072, 3584, 4096, 4352
SM_KI, SM_WI, SM_GA = 0, 64, 68
O_QC, O_KVC, O_QR, O_KR, O_VR, O_GR = 0, 1024, 3072, 3584, 4096, 5120

LOG_G = [math.log1p(-(2.0 ** (-5.0 - h))) for h in range(R_HEADS)]


def _cparams(sem=None):
    return pltpu.CompilerParams(dimension_semantics=sem, vmem_limit_bytes=VMEM_LIMIT)


def _dot(a, b, precision=None):
    return jnp.dot(a, b, preferred_element_type=F32, precision=precision)


def _dot_nt(a, b, precision=None):
    return lax.dot_general(a, b, (((1,), (1,)), ((), ())), preferred_element_type=F32, precision=precision)


def _sigmoid(x):
    return 1.0 / (1.0 + jnp.exp(-x))


def _rms(x, g):
    return x * lax.rsqrt(jnp.mean(x * x, axis=-1, keepdims=True) + RMS_EPS) * g


FFN_TM, FFN_TF = 640, 512


def _ffn_kernel(x_ref, g_ref, wg_ref, wu_ref, wo_ref, o_ref, h_scr, acc_scr):
    j = pl.program_id(1)

    @pl.when(j == 0)
    def _():
        h_scr[...] = _rms(x_ref[...], g_ref[...]).astype(BF16)
        acc_scr[...] = jnp.zeros_like(acc_scr)

    h = h_scr[...]
    gate = _dot(h, wg_ref[...])
    up = _dot(h, wu_ref[...])
    act = gate * _sigmoid(gate) * up
    acc_scr[...] += _dot(act.astype(BF16), wo_ref[...])

    @pl.when(j == pl.num_programs(1) - 1)
    def _():
        o_ref[...] = x_ref[...] + 0.5 * acc_scr[...]


def _ffn_half(x, g, w_in, w_out):
    nj = D_FF // FFN_TF
    return pl.pallas_call(
        _ffn_kernel,
        grid=(N_TOK // FFN_TM, nj),
        in_specs=[
            pl.BlockSpec((FFN_TM, D_MODEL), lambda i, j: (i, 0)),
            pl.BlockSpec((1, D_MODEL), lambda i, j: (0, 0)),
            pl.BlockSpec((D_MODEL, FFN_TF), lambda i, j: (0, j)),
            pl.BlockSpec((D_MODEL, FFN_TF), lambda i, j: (0, j + nj)),
            pl.BlockSpec((FFN_TF, D_MODEL), lambda i, j: (j, 0)),
        ],
        out_specs=pl.BlockSpec((FFN_TM, D_MODEL), lambda i, j: (i, 0)),
        out_shape=jax.ShapeDtypeStruct((N_TOK, D_MODEL), F32),
        scratch_shapes=[pltpu.VMEM((FFN_TM, D_MODEL), BF16), pltpu.VMEM((FFN_TM, D_MODEL), F32)],
        compiler_params=_cparams(("parallel", "arbitrary")),
        name="ffn_half",
    )(x, g.reshape(1, D_MODEL), w_in, w_in, w_out)


PROJ_TM = 640


def _proj_in_kernel(x_ref, g_ref, w_ref, z_ref, zb_ref, h_scr):
    @pl.when(pl.program_id(1) == 0)
    def _():
        h_scr[...] = _rms(x_ref[...], g_ref[...]).astype(BF16)

    z = _dot(h_scr[...], w_ref[...])
    z_ref[...] = z
    zb_ref[...] = z.astype(BF16)


def _proj_in(x, g, w, tn):
    n = w.shape[1]
    return pl.pallas_call(
        _proj_in_kernel,
        grid=(N_TOK // PROJ_TM, n // tn),
        in_specs=[
            pl.BlockSpec((PROJ_TM, D_MODEL), lambda i, j: (i, 0)),
            pl.BlockSpec((1, D_MODEL), lambda i, j: (0, 0)),
            pl.BlockSpec((D_MODEL, tn), lambda i, j: (0, j)),
        ],
        out_specs=[pl.BlockSpec((PROJ_TM, tn), lambda i, j: (i, j)),
                   pl.BlockSpec((PROJ_TM, tn), lambda i, j: (i, j))],
        out_shape=[jax.ShapeDtypeStruct((N_TOK, n), F32), jax.ShapeDtypeStruct((N_TOK, n), BF16)],
        scratch_shapes=[pltpu.VMEM((PROJ_TM, D_MODEL), BF16)],
        compiler_params=_cparams(("parallel", "arbitrary")),
        name="proj_in",
    )(x, g.reshape(1, D_MODEL), w)


OUT_TM = 320


def _proj_out_kernel(x_ref, a_ref, b_ref, wa_ref, wb_ref, o_ref):
    o_ref[...] = (x_ref[...] + _dot(a_ref[...].astype(BF16), wa_ref[...])
                  + _dot(b_ref[...].astype(BF16), wb_ref[...]))


def _proj_out(x, a, b, wa, wb):
    ka, kb = a.shape[1], b.shape[1]
    return pl.pallas_call(
        _proj_out_kernel,
        grid=(N_TOK // OUT_TM,),
        in_specs=[
            pl.BlockSpec((OUT_TM, D_MODEL), lambda i: (i, 0)),
            pl.BlockSpec((OUT_TM, ka), lambda i: (i, 0)),
            pl.BlockSpec((OUT_TM, kb), lambda i: (i, 0)),
            pl.BlockSpec((ka, D_MODEL), lambda i: (0, 0)),
            pl.BlockSpec((kb, D_MODEL), lambda i: (0, 0)),
        ],
        out_specs=pl.BlockSpec((OUT_TM, D_MODEL), lambda i: (i, 0)),
        out_shape=jax.ShapeDtypeStruct((N_TOK, D_MODEL), F32),
        compiler_params=_cparams(("parallel",)),
        name="proj_out",
    )(x, a, b, wa, wb)


def _ple_kernel(x_ref, p_ref, g_ref, wg_ref, wp_ref, gf_ref, o_ref, *, final):
    x = x_ref[...]
    gate = _sigmoid(_dot(_rms(x, g_ref[...]).astype(BF16), wg_ref[...]))
    y = x + gate * _dot(p_ref[...].astype(BF16), wp_ref[...])
    o_ref[...] = _rms(y, gf_ref[...]) if final else y


def _ple(x, p, g, w_gate, w_proj, g_final, final):
    return pl.pallas_call(
        functools.partial(_ple_kernel, final=final),
        grid=(N_TOK // OUT_TM,),
        in_specs=[
            pl.BlockSpec((OUT_TM, D_MODEL), lambda i: (i, 0)),
            pl.BlockSpec((OUT_TM, PLE_DIM), lambda i: (i, 0)),
            pl.BlockSpec((1, D_MODEL), lambda i: (0, 0)),
            pl.BlockSpec((D_MODEL, D_MODEL), lambda i: (0, 0)),
            pl.BlockSpec((PLE_DIM, D_MODEL), lambda i: (0, 0)),
            pl.BlockSpec((1, D_MODEL), lambda i: (0, 0)),
        ],
        out_specs=pl.BlockSpec((OUT_TM, D_MODEL), lambda i: (i, 0)),
        out_shape=jax.ShapeDtypeStruct((N_TOK, D_MODEL), F32),
        compiler_params=_cparams(("parallel",)),
        name="ple_add",
    )(x, p, g.reshape(1, D_MODEL), w_gate, w_proj, g_final.reshape(1, D_MODEL))


def _t5_bucket_np(dist):
    dist = np.maximum(np.asarray(dist, np.int64), 0)
    exact = N_BUCKETS // 2
    out = {}
    for dt in (np.float32, np.float64):
        scaled = np.log(np.maximum(dist, exact).astype(dt) / dt(exact)) / dt(math.log(128 / exact))
        large = np.minimum(exact + (scaled * dt(N_BUCKETS - exact)).astype(np.int64), N_BUCKETS - 1)
        out[dt] = np.where(dist < exact, dist, large)
    assert np.array_equal(out[np.float32], out[np.float64])
    return out[np.float32].astype(np.int32)


def _cmp_block_of_col(col):
    return 2 * (col % 32) + col // 32


def _bias_kernel(tbl_ref, bk_near, bk_cmp, bk_dec, bk_decc, bk_dil, o_near, o_cmp, o_dec, o_decc, o_dil):
    def lookup(bucket, h):
        acc = jnp.zeros(bucket.shape, F32)
        for b in range(N_BUCKETS):
            acc = jnp.where(bucket == b, tbl_ref[b, h], acc)
        return acc

    for rel in range(3):
        bk = bk_near[rel]
        for h in range(16):
            o_near[rel, h] = lookup(bk, h)
    bk = bk_cmp[...]
    for h in range(8):
        o_cmp[h] = lookup(bk, h)
    bk = bk_dec[...]
    for h in range(16):
        o_dec[h:h + 1, :] = lookup(bk, h)
    bk = bk_decc[...]
    for h in range(8):
        o_decc[h:h + 1, :] = lookup(bk, h)
    lane = lax.broadcasted_iota(jnp.int32, (QB, 8), 1)
    for cfg in range(3):
        bk = bk_dil[cfg]
        acc = jnp.zeros((QB, 8), F32)
        for h in range(8):
            acc = jnp.where(lane == h, lookup(bk, h), acc)
        o_dil[cfg] = acc


DEC_W = 2 * 2176


def _bias_tables(rel_bias):
    r = np.arange(QB)[:, None]
    c = np.arange(QB)[None, :]
    bk_near = np.stack([_t5_bucket_np(rel * QB + r - c) for rel in range(3)])
    t = np.arange(SEQ)[:, None]
    col = np.arange(N_CMP)[None, :]
    bk_cmp = _t5_bucket_np(t - (CMP_BLOCK * _cmp_block_of_col(col) + CMP_BLOCK - 1))
    bk_dec = _t5_bucket_np(PAST_LEN - np.arange(DEC_W)[None, :] // 2)
    bk_decc = _t5_bucket_np(PAST_LEN - (CMP_BLOCK * _cmp_block_of_col(col) + CMP_BLOCK - 1))
    i = np.arange(QB)[:, None]
    bk_dil = np.stack([np.broadcast_to(_t5_bucket_np(dil * (QB - i)), (QB, 8)) for dil in (1, 4, 16)])
    vm = pl.BlockSpec(memory_space=pltpu.VMEM)
    return pl.pallas_call(
        _bias_kernel,
        in_specs=[pl.BlockSpec(memory_space=pltpu.SMEM), vm, vm, vm, vm, vm],
        out_specs=[vm] * 5,
        out_shape=[
            jax.ShapeDtypeStruct((3, 16, QB, QB), F32),
            jax.ShapeDtypeStruct((8, SEQ, N_CMP), F32),
            jax.ShapeDtypeStruct((16, DEC_W), F32),
            jax.ShapeDtypeStruct((8, N_CMP), F32),
            jax.ShapeDtypeStruct((3, QB, 8), F32),
        ],
        compiler_params=_cparams(),
        name="bias_tables",
    )(rel_bias, jnp.asarray(bk_near), jnp.asarray(bk_cmp), jnp.asarray(bk_dec), jnp.asarray(bk_decc),
      jnp.asarray(bk_dil))


def _compress_rows(rows_ref, pe_ref, w_ref, out_ref, nblk):
    for c in range(2):
        acc = jnp.zeros((2 * nblk, HD), F32)
        for l in range(CMP_BLOCK):
            pe = pe_ref[c, l:l + 1, :]
            xs = [rows_ref[pl.ds(4 * l + 2 * g + c, nblk, stride=4 * CMP_BLOCK), :] + pe for g in range(2)]
            acc = acc + _dot(jnp.concatenate(xs, axis=0).astype(BF16), w_ref[c, l])
        for g in range(2):
            out_ref[:, (2 * g + c) * HD:(2 * g + c + 1) * HD] = acc[g * nblk:(g + 1) * nblk]


CMP_ROWS_P = 4096


def _compress_prompt_kernel(rows_ref, pe_ref, w_ref, out_ref):
    _compress_rows(rows_ref, pe_ref, w_ref, out_ref, CMP_ROWS_P // CMP_BLOCK)


def _compress_prompt(z, pe, w):
    nblk = CMP_ROWS_P // CMP_BLOCK
    rows = z[:N_TOK_P, E_CMP:E_SEL].reshape(4 * N_TOK_P, HD)
    return pl.pallas_call(
        _compress_prompt_kernel,
        grid=(N_TOK_P // CMP_ROWS_P,),
        in_specs=[
            pl.BlockSpec((4 * CMP_ROWS_P, HD), lambda i: (i, 0)),
            pl.BlockSpec((2, CMP_BLOCK, HD), lambda i: (0, 0, 0)),
            pl.BlockSpec((2, CMP_BLOCK, HD, HD), lambda i: (0, 0, 0, 0)),
        ],
        out_specs=pl.BlockSpec((nblk, 512), lambda i: (i, 0)),
        out_shape=jax.ShapeDtypeStruct((N_TOK_P // CMP_BLOCK, 512), F32),
        compiler_params=_cparams(("parallel",)),
        name="nsa_compress_prompt",
    )(rows, pe, w)


CMP_SEQS = 2


def _compress_sample_kernel(pt_ref, *refs):
    pages = refs[:CMP_SEQS * N_PAGES]
    pe_ref, w_ref, out_ref, rows_scr = refs[CMP_SEQS * N_PAGES:]
    for k, pg in enumerate(pages):
        rows_scr[4 * k * PAGE:4 * (k + 1) * PAGE, :] = pg[...]
    _compress_rows(rows_scr, pe_ref, w_ref, out_ref, CMP_SEQS * N_CMP)


def _page_map(s_local, k, n_seqs, i, pt_ref):
    return (pt_ref[i * n_seqs + s_local, k], 0, 0)


def _compress_sample(pool, page_table, pe, w):
    page_specs = [pl.BlockSpec((None, 4 * PAGE, HD), functools.partial(_page_map, s, k, CMP_SEQS))
                  for s in range(CMP_SEQS) for k in range(N_PAGES)]
    nblk = CMP_SEQS * N_CMP
    grid_spec = pltpu.PrefetchScalarGridSpec(
        num_scalar_prefetch=1,
        grid=(DEC_BATCH // CMP_SEQS,),
        in_specs=page_specs + [
            pl.BlockSpec((2, CMP_BLOCK, HD), lambda i, pt: (0, 0, 0)),
            pl.BlockSpec((2, CMP_BLOCK, HD, HD), lambda i, pt: (0, 0, 0, 0)),
        ],
        out_specs=pl.BlockSpec((nblk, 512), lambda i, pt: (i, 0)),
        scratch_shapes=[pltpu.VMEM((4 * CMP_SEQS * PAST_LEN, HD), F32)],
    )
    return pl.pallas_call(
        _compress_sample_kernel,
        grid_spec=grid_spec,
        out_shape=jax.ShapeDtypeStruct((DEC_BATCH * N_CMP, 512), F32),
        compiler_params=_cparams(("parallel",)),
        name="nsa_compress_sample",
    )(page_table, *([pool] * (CMP_SEQS * N_PAGES)), pe, w)


def _even_first(cmp_sum, n_seq):
    x = cmp_sum.reshape(n_seq, N_CMP // 2, 2, 512)
    return jnp.swapaxes(x, 1, 2).reshape(n_seq, N_CMP, 512)


RG = 512
N_RG = SEQ // RG
RG_CHUNKS = RG // QB


def _attend_regions(i, q_bf, k_ref, kcol, v_ref, vcol, logit_fn, st, min_chunk=None):
    m_scr, l_scr, acc_scr = st
    m_scr[...] = jnp.full(m_scr.shape, NEG_INF, F32)
    l_scr[...] = jnp.zeros(l_scr.shape, F32)
    acc_scr[...] = jnp.zeros(acc_scr.shape, F32)
    for rg in range(N_RG):
        visit = rg * RG_CHUNKS <= i
        if min_chunk is not None:
            visit = visit & ((rg + 1) * RG_CHUNKS - 1 >= min_chunk)

        @pl.when(visit)
        def _():
            s, weight = logit_fn(rg, _dot_nt(q_bf, k_ref[rg * RG:(rg + 1) * RG, kcol:kcol + HD]))
            m_old = m_scr[...]
            m_new = jnp.maximum(m_old, jnp.max(s, axis=-1, keepdims=True))
            alpha = jnp.exp(m_old - m_new)
            p = jnp.exp(s - m_new)
            if weight is not None:
                p = p * weight
            l_scr[...] = alpha * l_scr[...] + jnp.sum(p, axis=-1, keepdims=True)
            acc_scr[...] = alpha * acc_scr[...] + _dot(p.astype(BF16), v_ref[rg * RG:(rg + 1) * RG, vcol:vcol + HD])
            m_scr[...] = m_new

    return acc_scr[...] / jnp.maximum(l_scr[...], 1e-30)


def _softmax_state(rows):
    return [pltpu.VMEM((rows, 1), F32), pltpu.VMEM((rows, 1), F32), pltpu.VMEM((rows, HD), F32)]


def _mask_rows4(ok, logits):
    w = logits.shape[1]
    return jnp.where(ok[None], logits.reshape(4, QB, w), NEG_INF).reshape(4 * QB, w)


def _stack_heads(q_ref, g):
    return jnp.concatenate([q_ref[:, (4 * g + a) * HD:(4 * g + a + 1) * HD] for a in range(4)], axis=0)


def _rel_dist(i, kb):
    r = lax.broadcasted_iota(jnp.int32, (QB, QB), 0)
    c = lax.broadcasted_iota(jnp.int32, (QB, QB), 1)
    return (i - kb) * QB + r - c


def _region_dist(i, rg):
    r = lax.broadcasted_iota(jnp.int32, (QB, RG), 0)
    c = lax.broadcasted_iota(jnp.int32, (QB, RG), 1)
    return i * QB - rg * RG + r - c


def _region_bias(bnear_ref, i, rg, heads):
    tiles = [bnear_ref[jnp.clip(i - (rg * RG_CHUNKS + c), 0, 2), heads].reshape(-1, QB) for c in range(RG_CHUNKS)]
    return jnp.concatenate(tiles, axis=1)


def _split_bf16(a):
    hi = a.astype(BF16)
    return hi, (a - hi.astype(F32)).astype(BF16)


def _dot_nt3(a, b):
    ah, al = _split_bf16(a)
    bh, bl = _split_bf16(b)
    return _dot_nt(ah, bh) + (_dot_nt(ah, bl) + _dot_nt(al, bh))


def _rank_select(score, n_blocks, n_take):
    blk = lax.broadcasted_iota(jnp.int32, score.shape, 1)
    rank = jnp.zeros(score.shape, F32)
    for j in range(n_blocks):
        col = score[:, j:j + 1]
        beats = (col > score) | ((col == score) & (blk > j))
        rank = rank + jnp.where(beats, 1.0, 0.0)
    return jnp.where(rank < n_take, 1.0, 0.0)


def _nsa_prompt_kernel(q_ref, sm_ref, ksel_ref, kwin_ref, cmp_ref, bnear_ref, bcmp_ref, o_ref,
                       selk_scr, m_scr, l_scr, acc_scr):
    i = pl.program_id(1)
    st = (m_scr, l_scr, acc_scr)
    sm = sm_ref[...]
    jj = lax.broadcasted_iota(jnp.int32, (N_CMP // 2, SEQ), 0)
    cc = lax.broadcasted_iota(jnp.int32, (N_CMP // 2, SEQ), 1)
    expand = jnp.where(jj == cc // SEL_BLOCK, 1.0, 0.0).astype(BF16)
    row = lax.broadcasted_iota(jnp.int32, (4 * QB, N_CMP), 0) & (QB - 1)
    col = lax.broadcasted_iota(jnp.int32, (4 * QB, N_CMP), 1)
    dist_c = i * QB + row - (CMP_BLOCK * _cmp_block_of_col(col) + CMP_BLOCK - 1)
    ok_c = dist_c >= 0
    t_q = i * QB + lax.broadcasted_iota(jnp.int32, (QB, N_CMP // 2), 0)
    blk = lax.broadcasted_iota(jnp.int32, (QB, N_CMP // 2), 1)
    cur = t_q // SEL_BLOCK
    forced = (blk == 0) | (blk == cur) | (blk == cur - 1)
    admissible = blk * SEL_BLOCK <= t_q
    for g in range(2):
        q = _stack_heads(q_ref, g) * Q_SCALE
        q_bf = q.astype(BF16)
        heads = slice(4 * g, 4 * g + 4)
        logit = _dot_nt3(q, cmp_ref[:, 2 * g * HD:(2 * g + 1) * HD]) + bcmp_ref[4 * g:4 * g + 4].reshape(4 * QB, N_CMP)
        logit = jnp.where(ok_c, logit, NEG_INF)
        e = jnp.where(ok_c, jnp.exp(logit - jnp.max(logit, axis=-1, keepdims=True)), 0.0)
        p_c = e / jnp.maximum(jnp.sum(e, axis=-1, keepdims=True), 1e-30)
        o_c = _dot(p_c.astype(BF16), cmp_ref[:, (2 * g + 1) * HD:(2 * g + 2) * HD].astype(BF16))
        p_g = p_c[0:QB] + p_c[QB:2 * QB] + p_c[2 * QB:3 * QB] + p_c[3 * QB:4 * QB]
        score = p_g[:, :N_CMP // 2] + p_g[:, N_CMP // 2:]
        score = jnp.where(forced, FORCE_SCORE, score)
        score = jnp.where(admissible, score, NEG_INF)
        sel = _rank_select(score, SEQ // SEL_BLOCK, N_SEL).astype(BF16)
        selk_scr[...] = _dot(sel, expand)

        def sel_logits(rg, raw):
            ok = (selk_scr[:, rg * RG:(rg + 1) * RG] > 0.5) & (_region_dist(i, rg) >= 0)
            return _mask_rows4(ok, raw + _region_bias(bnear_ref, i, rg, heads)), None

        o_s = _attend_regions(i, q_bf, ksel_ref, 2 * g * HD, ksel_ref, (2 * g + 1) * HD, sel_logits, st)

        def win_logits(rg, raw):
            d = _region_dist(i, rg)
            return _mask_rows4((d >= 0) & (d <= A_WINDOW), raw + _region_bias(bnear_ref, i, rg, heads)), None

        o_w = _attend_regions(i, q_bf, kwin_ref, 2 * g * HD, kwin_ref, (2 * g + 1) * HD, win_logits, st,
                              min_chunk=i - A_WINDOW // QB)
        gates = []
        for br in range(3):
            gates.append(jnp.concatenate(
                [_sigmoid(sm[:, SM_GA + 3 * (4 * g + a) + br:SM_GA + 3 * (4 * g + a) + br + 1]) for a in range(4)],
                axis=0))
        out = gates[0] * o_c + gates[1] * o_s + gates[2] * o_w
        for a in range(4):
            o_ref[:, (4 * g + a) * HD:(4 * g + a + 1) * HD] = out[a * QB:(a + 1) * QB]


def _nsa_prompt(z, zb, cmp_sum, bnear, bcmp):
    return pl.pallas_call(
        _nsa_prompt_kernel,
        grid=(BATCH, NQB),
        in_specs=[
            pl.BlockSpec((QB, 1024), lambda b, i: (b * NQB + i, E_QA // 1024)),
            pl.BlockSpec((QB, 128), lambda b, i: (b * NQB + i, E_SM // 128)),
            pl.BlockSpec((SEQ, 512), lambda b, i: (b, E_SEL // 512)),
            pl.BlockSpec((SEQ, 512), lambda b, i: (b, E_WIN // 512)),
            pl.BlockSpec((None, N_CMP, 512), lambda b, i: (b, 0, 0)),
            pl.BlockSpec((3, 8, QB, QB), lambda b, i: (0, 0, 0, 0)),
            pl.BlockSpec((8, QB, N_CMP), lambda b, i: (0, i, 0)),
        ],
        out_specs=pl.BlockSpec((QB, 1024), lambda b, i: (b * NQB + i, 0)),
        out_shape=jax.ShapeDtypeStruct((N_TOK_P, 1024), F32),
        scratch_shapes=[pltpu.VMEM((QB, SEQ), F32)] + _softmax_state(4 * QB),
        compiler_params=_cparams(("parallel", "arbitrary")),
        name="nsa_prompt",
    )(z, z, zb, zb, cmp_sum, bnear, bcmp)


def _dsa_prompt_kernel(q_ref, qi_ref, sm_ref, kidx_ref, kv_ref, bnear_ref, o_ref,
                       sc_scr, mk_scr, m_scr, l_scr, acc_scr):
    i = pl.program_id(1)
    sm = sm_ref[...]
    qi = qi_ref[...]
    wi = sm[:, SM_WI:SM_WI + 4] * 0.5

    for rg in range(N_RG):
        cols = slice(rg * RG, (rg + 1) * RG)

        @pl.when(rg * RG_CHUNKS <= i)
        def _():
            kid = kidx_ref[cols, SM_KI:SM_KI + 64]
            sc = jnp.zeros((QB, RG), F32)
            for h in range(4):
                sc = sc + wi[:, h:h + 1] * jnp.maximum(_dot_nt3(qi[:, 64 * h:64 * h + 64], kid), 0.0)
            sc_scr[:, cols] = jnp.where(_region_dist(i, rg) >= 0, sc, NEG_INF)

        @pl.when(rg * RG_CHUNKS > i)
        def _():
            sc_scr[:, cols] = jnp.full((QB, RG), NEG_INF, F32)

    scores = sc_scr[...]
    lo = jnp.min(jnp.where(scores > 0.5 * NEG_INF, scores, BIG), axis=-1, keepdims=True)
    hib = jnp.max(scores, axis=-1, keepdims=True)
    n_causal = (i * QB + 1 + lax.broadcasted_iota(jnp.int32, (QB, 1), 0)).astype(F32)
    k_eff = jnp.minimum(n_causal, float(IDX_TOPK))

    def count(thr, strict):
        s = sc_scr[...]
        return jnp.sum(jnp.where((s > thr) if strict else (s >= thr), 1.0, 0.0), axis=-1, keepdims=True)

    def max_below(bound):
        s = sc_scr[...]
        return jnp.max(jnp.where(s < bound, s, -BIG), axis=-1, keepdims=True)

    hix = jnp.full((QB, 1), BIG, F32)
    for _ in range(BISECT_STEPS):
        mid = 0.5 * (lo + hib)
        ge = count(mid, False) >= k_eff
        lo = jnp.where(ge, mid, lo)
        hib = jnp.where(ge, hib, mid)
        hix = jnp.where(ge, hix, mid)

    def step_cond(c):
        return c[3] > 0.5

    def step_body(c):
        bound, thr, done, _ = c
        cand = max_below(bound)
        ok = count(cand, False) >= k_eff
        active = done < 0.5
        thr = jnp.where(active & ok, cand, thr)
        bound = jnp.where(active & jnp.logical_not(ok), cand, bound)
        done = jnp.where(ok, 1.0, done)
        return bound, thr, done, jnp.sum(1.0 - done)

    _, thr, _, _ = lax.while_loop(step_cond, step_body,
                                  (hix, lo, jnp.zeros((QB, 1), F32), jnp.float32(QB)))
    need = k_eff - count(thr, True)
    upper = jnp.where(lax.broadcasted_iota(jnp.int32, (QB, QB), 0) <= lax.broadcasted_iota(jnp.int32, (QB, QB), 1),
                      1.0, 0.0).astype(BF16)

    carry = jnp.zeros((QB, 1), F32)
    for c in range(NQB):
        s = sc_scr[:, c * QB:(c + 1) * QB]
        eq = s == thr
        eqf = jnp.where(eq, 1.0, 0.0)
        prefix = _dot(eqf.astype(BF16), upper) + carry
        mk_scr[:, c * QB:(c + 1) * QB] = jnp.where((s > thr) | (eq & (prefix <= need)), 1.0, 0.0)
        carry = carry + jnp.sum(eqf, axis=-1, keepdims=True)

    for g in range(2):
        q_bf = (_stack_heads(q_ref, g) * Q_SCALE).astype(BF16)

        def logits(rg, raw):
            bias = _region_bias(bnear_ref, i, rg, slice(4 * g, 4 * g + 4))
            return _mask_rows4(mk_scr[:, rg * RG:(rg + 1) * RG] > 0.5, raw + bias), None

        out = _attend_regions(i, q_bf, kv_ref, 2 * g * HD, kv_ref, (2 * g + 1) * HD, logits,
                              (m_scr, l_scr, acc_scr))
        for a in range(4):
            o_ref[:, (4 * g + a) * HD:(4 * g + a + 1) * HD] = out[a * QB:(a + 1) * QB]


def _dsa_prompt(z, zb, bnear):
    return pl.pallas_call(
        _dsa_prompt_kernel,
        grid=(BATCH, NQB),
        in_specs=[
            pl.BlockSpec((QB, 1024), lambda b, i: (b * NQB + i, E_QB // 1024)),
            pl.BlockSpec((QB, 256), lambda b, i: (b * NQB + i, E_QI // 256)),
            pl.BlockSpec((QB, 128), lambda b, i: (b * NQB + i, E_SM // 128)),
            pl.BlockSpec((SEQ, 128), lambda b, i: (b, E_SM // 128)),
            pl.BlockSpec((SEQ, 512), lambda b, i: (b, E_KVB // 512)),
            pl.BlockSpec((3, 8, QB, QB), lambda b, i: (0, 0, 0, 0)),
        ],
        out_specs=pl.BlockSpec((QB, 1024), lambda b, i: (b * NQB + i, 0)),
        out_shape=jax.ShapeDtypeStruct((N_TOK_P, 1024), F32),
        scratch_shapes=[pltpu.VMEM((QB, SEQ), F32), pltpu.VMEM((QB, SEQ), F32)] + _softmax_state(4 * QB),
        compiler_params=_cparams(("parallel", "arbitrary")),
        name="dsa_prompt",
    )(z, z, z, z, zb, bnear)


DIL_CFG = ((128, 1), (512, 4), (2048, 16))
DIL_REL = 5


def _dil_count(dist):
    cnt = jnp.zeros(dist.shape, F32)
    for window, dil in DIL_CFG:
        cnt = cnt + jnp.where((dist >= 0) & (dist <= window) & ((dist & (dil - 1)) == 0), 1.0, 0.0)
    return cnt


def _dil_prompt_kernel(q_ref, kva_ref, kvb_ref, bnear_ref, o_ref, cnt_scr, m_scr, l_scr, acc_scr):
    i = pl.program_id(1)
    for rel in range(DIL_REL + 1):
        cnt_scr[rel] = _dil_count(_rel_dist(rel, 0))
    cnt_scr[DIL_REL + 1] = jnp.zeros((QB, QB), F32)

    def region_count(rg):
        tiles = []
        for c in range(RG_CHUNKS):
            rel = i - (rg * RG_CHUNKS + c)
            tiles.append(cnt_scr[jnp.where(rel < 0, DIL_REL + 1, jnp.minimum(rel, DIL_REL))])
        return jnp.concatenate(tiles, axis=1)

    for h in range(8):
        kv_ref = kva_ref if h < 4 else kvb_ref
        col = (h % 4) * 2 * HD
        q_bf = (q_ref[:, h * HD:(h + 1) * HD] * Q_SCALE).astype(BF16)

        def logits(rg, raw):
            cnt = region_count(rg)
            return jnp.where(cnt > 0.5, raw + _region_bias(bnear_ref, i, rg, h), NEG_INF), cnt

        o_ref[:, h * HD:(h + 1) * HD] = _attend_regions(i, q_bf, kv_ref, col, kv_ref, col + HD, logits,
                                                        (m_scr, l_scr, acc_scr))


def _dil_prompt(z, zb, bnear):
    return pl.pallas_call(
        _dil_prompt_kernel,
        grid=(BATCH, NQB),
        in_specs=[
            pl.BlockSpec((QB, 1024), lambda b, i: (b * NQB + i, O_QC // 1024)),
            pl.BlockSpec((SEQ, 1024), lambda b, i: (b, O_KVC // 1024)),
            pl.BlockSpec((SEQ, 1024), lambda b, i: (b, O_KVC // 1024 + 1)),
            pl.BlockSpec((3, 8, QB, QB), lambda b, i: (0, 0, 0, 0)),
        ],
        out_specs=pl.BlockSpec((QB, 1024), lambda b, i: (b * NQB + i, 0)),
        out_shape=jax.ShapeDtypeStruct((N_TOK_P, 1024), F32),
        scratch_shapes=[pltpu.VMEM((DIL_REL + 2, QB, QB), F32)] + _softmax_state(QB),
        compiler_params=_cparams(("parallel", "arbitrary")),
        name="dilated_prompt",
    )(z, zb, zb, bnear)


def _rope_rows(x, cos2, sin2):
    return x * cos2 + pltpu.roll(x, 64, 1) * sin2


def _readout(o, gate, gn):
    mu = jnp.mean(o, axis=-1, keepdims=True)
    d = o - mu
    var = jnp.mean(d * d, axis=-1, keepdims=True)
    return gate * _sigmoid(gate) * (d * lax.rsqrt(var + RMS_EPS)) * gn


def _ret_prompt_kernel(q_ref, k_ref, v_ref, gr_ref, cos_ref, sin_ref, gn_ref, o_ref, st_ref, st_scr):
    ci = pl.program_id(1)

    @pl.when(ci == 0)
    def _():
        st_scr[...] = jnp.zeros_like(st_scr)

    cos2, sin2 = cos_ref[...], sin_ref[...]
    r = lax.broadcasted_iota(jnp.int32, (QB, QB), 0)
    c = lax.broadcasted_iota(jnp.int32, (QB, QB), 1)
    rel = (r - c).astype(F32)
    pos_col = (lax.broadcasted_iota(jnp.int32, (QB, 1), 0)).astype(F32)
    for h in range(R_HEADS):
        lg = LOG_G[h]
        q = _rope_rows(q_ref[:, h * R_DK:(h + 1) * R_DK], cos2, sin2)
        k = _rope_rows(k_ref[:, h * R_DK:(h + 1) * R_DK], cos2, sin2) * RK_SCALE
        v = v_ref[:, h * R_DV:(h + 1) * R_DV]
        state = st_scr[h]
        decay = jnp.where(r >= c, jnp.exp(lg * jnp.maximum(rel, 0.0)), 0.0)
        q_bf, v_bf = q.astype(BF16), v.astype(BF16)
        scores = _dot_nt(q_bf, k.astype(BF16)) * decay
        out = _dot(scores.astype(BF16), v_bf)
        out = out + _dot(q_bf, state.astype(BF16)) * jnp.exp(lg * (pos_col + 1.0))
        zeta = jnp.exp(lg * (QB - 1.0 - pos_col))
        new_state = state * math.exp(lg * QB) + _dot((k * zeta).T.astype(BF16), v_bf)
        st_scr[h] = new_state
        st_ref[h] = new_state
        o_ref[:, h * R_DV:(h + 1) * R_DV] = _readout(out, gr_ref[:, h * R_DV:(h + 1) * R_DV],
                                                     gn_ref[:, h * R_DV:(h + 1) * R_DV])


def _ret_prompt(z, cos2, sin2, gn):
    return pl.pallas_call(
        _ret_prompt_kernel,
        grid=(BATCH, NQB),
        in_specs=[
            pl.BlockSpec((QB, 512), lambda b, i: (b * NQB + i, O_QR // 512)),
            pl.BlockSpec((QB, 512), lambda b, i: (b * NQB + i, O_KR // 512)),
            pl.BlockSpec((QB, 1024), lambda b, i: (b * NQB + i, O_VR // 1024)),
            pl.BlockSpec((QB, 1024), lambda b, i: (b * NQB + i, O_GR // 1024)),
            pl.BlockSpec((QB, 128), lambda b, i: (i, 0)),
            pl.BlockSpec((QB, 128), lambda b, i: (i, 0)),
            pl.BlockSpec((1, 1024), lambda b, i: (0, 0)),
        ],
        out_specs=[pl.BlockSpec((QB, 1024), lambda b, i: (b * NQB + i, 0)),
                   pl.BlockSpec((None, R_HEADS, R_DK, R_DV), lambda b, i: (b, 0, 0, 0))],
        out_shape=[jax.ShapeDtypeStruct((N_TOK_P, 1024), F32),
                   jax.ShapeDtypeStruct((BATCH, R_HEADS, R_DK, R_DV), F32)],
        scratch_shapes=[pltpu.VMEM((R_HEADS, R_DK, R_DV), F32)],
        compiler_params=_cparams(("parallel", "arbitrary")),
        name="retention_prompt",
    )(z, z, z, z, cos2, sin2, gn.reshape(1, 1024))


DI_QA, DI_QB, DI_KSEL, DI_VSEL, DI_KWIN, DI_VWIN, DI_KB, DI_VB, DI_QI, DI_KI, DI_MISC = (
    0, 128, 256, 384, 512, 640, 768, 896, 1024, 1088, 1152)
DEC_IN_W = 1280
DEC_SEL_BLOCKS = PAST_LEN // SEL_BLOCK + 1


def _kv_rows(ref, start, kv):
    return ref[pl.ds(4 * start + kv, 2 * PAGE, stride=2), :].astype(BF16)


def _decode_attend(q, q_bf, chunks, k_new, v_new, bias_new, new_ok, logit_fn):
    row = lax.broadcasted_iota(jnp.int32, (8, 2 * PAGE), 0)
    col = lax.broadcasted_iota(jnp.int32, (8, 2 * PAGE), 1)
    own = (col & 1) == jnp.where(row < 4, 0, 1)
    logits = []
    for k, (ref, start) in enumerate(chunks):
        logits.append(jnp.where(own, logit_fn(k, _dot_nt(q_bf, _kv_rows(ref, start, 0))), NEG_INF))
    s_new = jnp.sum(q * k_new, axis=-1, keepdims=True) + bias_new
    if new_ok is not None:
        s_new = jnp.where(new_ok, s_new, NEG_INF)
    m = s_new
    for s in logits:
        m = jnp.maximum(m, jnp.max(s, axis=-1, keepdims=True))
    p_new = jnp.exp(s_new - m)
    l = p_new
    acc = p_new * v_new
    for s, (ref, start) in zip(logits, chunks):
        p = jnp.exp(s - m)
        l = l + jnp.sum(p, axis=-1, keepdims=True)
        acc = acc + _dot(p.astype(BF16), _kv_rows(ref, start, 1))
    return acc / jnp.maximum(l, 1e-30)


def _even_sample_kernel(pt_ref, *refs):
    sel_pages = refs[0:N_PAGES]
    kvb_pages = refs[N_PAGES:2 * N_PAGES]
    idx_pages = refs[2 * N_PAGES:3 * N_PAGES]
    din_ref, cmp_ref, win_ref, expand_ref, dec_ref, decc_ref, o_ref = refs[3 * N_PAGES:]
    d = din_ref[...]
    row = lax.broadcasted_iota(jnp.int32, (8, QB), 0)
    row_c = lax.broadcasted_iota(jnp.int32, (8, N_CMP), 0)
    pick_c = lambda a, b: jnp.where(row_c < 4, a, b)
    pick = lambda a, b: jnp.where(row < 4, a, b)
    dec_a, dec_b = dec_ref[0:8, :], dec_ref[8:16, :]

    qa = d[:, DI_QA:DI_QA + HD] * Q_SCALE
    qa_bf = qa.astype(BF16)
    cmp = cmp_ref[...]
    logit = pick_c(_dot_nt3(qa, cmp[:, 0:HD]), _dot_nt3(qa, cmp[:, 2 * HD:3 * HD])) + decc_ref[...]
    e = jnp.exp(logit - jnp.max(logit, axis=-1, keepdims=True))
    p_c = e / jnp.maximum(jnp.sum(e, axis=-1, keepdims=True), 1e-30)
    p_bf = p_c.astype(BF16)
    cmp_bf = cmp.astype(BF16)
    o_c = pick(_dot(p_bf, cmp_bf[:, HD:2 * HD]), _dot(p_bf, cmp_bf[:, 3 * HD:4 * HD]))
    g0 = jnp.sum(jnp.where(row_c < 4, p_c, 0.0), axis=0, keepdims=True)
    g1 = jnp.sum(jnp.where(row_c < 4, 0.0, p_c), axis=0, keepdims=True)
    p_g = pick_c(jnp.broadcast_to(g0, (8, N_CMP)), jnp.broadcast_to(g1, (8, N_CMP)))
    blk = lax.broadcasted_iota(jnp.int32, (8, N_CMP), 1)
    score = jnp.concatenate([p_g[:, :N_CMP // 2] + p_g[:, N_CMP // 2:], jnp.zeros((8, N_CMP // 2), F32)], axis=1)
    cur = PAST_LEN // SEL_BLOCK
    score = jnp.where((blk == 0) | (blk == cur) | (blk == cur - 1), FORCE_SCORE, score)
    score = jnp.where(blk < DEC_SEL_BLOCKS, score, -BIG)
    sel = _rank_select(score, DEC_SEL_BLOCKS, N_SEL) * jnp.where(blk < DEC_SEL_BLOCKS, 1.0, 0.0)
    sel_pos = _dot(sel.astype(BF16), expand_ref[...])
    cols = lambda k: slice(2 * k * PAGE, 2 * (k + 1) * PAGE)
    new_col = slice(2 * PAST_LEN, 2 * PAST_LEN + 1)

    def sel_logits(k, raw):
        return jnp.where(sel_pos[:, cols(k)] > 0.5, raw + dec_a[:, cols(k)], NEG_INF)

    o_s = _decode_attend(qa, qa_bf, [(pg, 0) for pg in sel_pages], d[:, DI_KSEL:DI_KSEL + HD],
                         d[:, DI_VSEL:DI_VSEL + HD], dec_a[:, new_col], None, sel_logits)
    win0 = (PAST_LEN - A_WINDOW) // PAGE

    def win_logits(k, raw):
        return raw + dec_a[:, cols(win0 + k)]

    o_w = _decode_attend(qa, qa_bf, [(win_ref, k * PAGE) for k in range(A_WINDOW // PAGE)],
                         d[:, DI_KWIN:DI_KWIN + HD], d[:, DI_VWIN:DI_VWIN + HD], dec_a[:, new_col], None, win_logits)
    gate = _sigmoid(d[:, DI_MISC:DI_MISC + 3])
    o_ref[:, 0:HD] = gate[:, 0:1] * o_c + gate[:, 1:2] * o_s + gate[:, 2:3] * o_w

    qi = d[:, DI_QI:DI_QI + 64]
    wi = d[:, DI_MISC + 3:DI_MISC + 4] * 0.5
    row16 = lax.broadcasted_iota(jnp.int32, (N_PAGES, PAGE), 0)
    sc = jnp.zeros((N_PAGES, PAGE), F32)
    for k, pg in enumerate(idx_pages):
        sc_k = jnp.sum(wi * jnp.maximum(_dot_nt3(qi, pg[...]), 0.0), axis=0, keepdims=True)
        sc = jnp.where(row16 == k, sc_k, sc)
    sc_new = jnp.sum(wi * jnp.maximum(jnp.sum(qi * d[:, DI_KI:DI_KI + 64], axis=-1, keepdims=True), 0.0),
                     axis=0, keepdims=True)
    total = lambda x: jnp.sum(jnp.sum(x, axis=-1, keepdims=True), axis=0, keepdims=True)

    def count(thr, strict):
        hit = (lambda x: x > thr) if strict else (lambda x: x >= thr)
        return total(jnp.where(hit(sc), 1.0, 0.0)) + jnp.where(hit(sc_new), 1.0, 0.0)

    def max_below(bound):
        best = jnp.max(jnp.max(jnp.where(sc < bound, sc, -BIG), axis=-1, keepdims=True), axis=0, keepdims=True)
        return jnp.maximum(best, jnp.where(sc_new < bound, sc_new, -BIG))

    k_top = float(IDX_TOPK)
    lo = jnp.minimum(jnp.min(jnp.min(sc, axis=-1, keepdims=True), axis=0, keepdims=True), sc_new)
    hib = jnp.maximum(jnp.max(jnp.max(sc, axis=-1, keepdims=True), axis=0, keepdims=True), sc_new)
    hix = jnp.full((1, 1), BIG, F32)
    for _ in range(BISECT_STEPS):
        mid = 0.5 * (lo + hib)
        ge = count(mid, False) >= k_top
        lo = jnp.where(ge, mid, lo)
        hib = jnp.where(ge, hib, mid)
        hix = jnp.where(ge, hix, mid)

    def step_body(c):
        bound, thr, _ = c
        cand = max_below(bound)
        ok = count(cand, False) >= k_top
        return jnp.where(ok, bound, cand), jnp.where(ok, cand, thr), jnp.sum(jnp.where(ok, 0.0, 1.0))

    _, thr, _ = lax.while_loop(lambda c: c[2] > 0.5, step_body, (hix, lo, jnp.float32(1.0)))
    n_gt = count(thr, True)
    need = k_top - n_gt
    eq = jnp.where(sc == thr, 1.0, 0.0).astype(BF16)
    r128 = lax.broadcasted_iota(jnp.int32, (QB, QB), 0)
    c128 = lax.broadcasted_iota(jnp.int32, (QB, QB), 1)
    upper = jnp.where(r128 <= c128, 1.0, 0.0).astype(BF16)
    r16 = lax.broadcasted_iota(jnp.int32, (N_PAGES, N_PAGES), 0)
    c16 = lax.broadcasted_iota(jnp.int32, (N_PAGES, N_PAGES), 1)
    lower = jnp.where(c16 < r16, 1.0, 0.0).astype(BF16)
    prefix = _dot(eq, upper) + jnp.sum(_dot(lower, eq), axis=-1, keepdims=True)
    keep = jnp.where((sc > thr) | ((sc == thr) & (prefix <= need)), 1.0, 0.0)
    n_eq_past = total(eq.astype(F32))
    new_ok = (sc_new > thr) | ((sc_new == thr) & (n_gt + n_eq_past + 1.0 <= k_top))
    pos = lax.broadcasted_iota(jnp.int32, (PAGE, 2 * PAGE), 0)
    col2 = lax.broadcasted_iota(jnp.int32, (PAGE, 2 * PAGE), 1)
    keep2 = _dot(keep.astype(BF16), jnp.where(col2 // 2 == pos, 1.0, 0.0).astype(BF16))

    qb = d[:, DI_QB:DI_QB + HD] * Q_SCALE

    def b_logits(k, raw):
        return jnp.where(keep2[k:k + 1, :] > 0.5, raw + dec_b[:, cols(k)], NEG_INF)

    o_ref[:, HD:2 * HD] = _decode_attend(
        qb, qb.astype(BF16), [(pg, 0) for pg in kvb_pages], d[:, DI_KB:DI_KB + HD], d[:, DI_VB:DI_VB + HD],
        dec_b[:, new_col], new_ok, b_logits)


def _seq_page_map(k, s, pt_ref):
    return (pt_ref[s, k], 0, 0)


def _even_sample(page_table, pool_sel, pool_b, pool_idx, din, cmp_sum, win_buf, expand, dec, decc):
    specs = []
    for shape in ((4 * PAGE, HD), (4 * PAGE, HD), (PAGE, 64)):
        specs += [pl.BlockSpec((None,) + shape, functools.partial(_seq_page_map, k)) for k in range(N_PAGES)]
    specs += [
        pl.BlockSpec((None, 8, DEC_IN_W), lambda s, pt: (s, 0, 0)),
        pl.BlockSpec((None, N_CMP, 512), lambda s, pt: (s, 0, 0)),
        pl.BlockSpec((None, 4 * A_WINDOW, HD), lambda s, pt: (s, 0, 0)),
        pl.BlockSpec((N_CMP, 2 * PAST_LEN), lambda s, pt: (0, 0)),
        pl.BlockSpec((16, DEC_W), lambda s, pt: (0, 0)),
        pl.BlockSpec((8, N_CMP), lambda s, pt: (0, 0)),
    ]
    grid_spec = pltpu.PrefetchScalarGridSpec(
        num_scalar_prefetch=1, grid=(DEC_BATCH,), in_specs=specs,
        out_specs=pl.BlockSpec((None, 8, 2 * HD), lambda s, pt: (s, 0, 0)))
    return pl.pallas_call(
        _even_sample_kernel,
        grid_spec=grid_spec,
        out_shape=jax.ShapeDtypeStruct((DEC_BATCH, 8, 2 * HD), F32),
        compiler_params=_cparams(("parallel",)),
        name="even_sample",
    )(page_table, *([pool_sel] * N_PAGES), *([pool_b] * N_PAGES), *([pool_idx] * N_PAGES),
      din, cmp_sum, win_buf, expand, dec, decc)


def _odd_sample_kernel(c0_ref, c1_ref, c2_ref, qkv_ref, rrow_ref, kcol_ref, st_ref, bdil_ref, tbl_ref,
                       rope_ref, ropec_ref, gn_ref, oc_ref, od_ref, st_out_ref):
    qkv = qkv_ref[...]
    q = qkv[:, 0:HD] * Q_SCALE
    lane = lax.broadcasted_iota(jnp.int32, (QB, 8), 1)
    lane1 = lax.broadcasted_iota(jnp.int32, (1, 8), 1)
    cfg_refs = (c0_ref, c1_ref, c2_ref)
    logits = []
    for cfg in range(3):
        s = bdil_ref[cfg]
        for h in range(8):
            s_h = jnp.sum(cfg_refs[cfg][:, 2 * h, :] * q[h:h + 1, :], axis=-1, keepdims=True)
            s = s + jnp.where(lane == h, s_h, 0.0)
        logits.append(s)
    s_new = tbl_ref[0:1, 0:8]
    for h in range(8):
        s_h = jnp.sum(q[h:h + 1, :] * qkv[h:h + 1, HD:2 * HD], axis=-1, keepdims=True)
        s_new = s_new + jnp.where(lane1 == h, s_h, 0.0)
    m = s_new
    for s in logits:
        m = jnp.maximum(m, jnp.max(s, axis=0, keepdims=True))
    p_new = 3.0 * jnp.exp(s_new - m)
    probs = [jnp.exp(s - m) for s in logits]
    l = p_new
    for p in probs:
        l = l + jnp.sum(p, axis=0, keepdims=True)
    inv = 1.0 / jnp.maximum(l, 1e-30)
    for h in range(8):
        acc = p_new[:, h:h + 1] * qkv[h:h + 1, 2 * HD:3 * HD]
        for cfg in range(3):
            v = cfg_refs[cfg][:, 2 * h + 1, :]
            acc = acc + jnp.sum(probs[cfg][:, h:h + 1] * v, axis=0, keepdims=True)
        oc_ref[:, h * HD:(h + 1) * HD] = acc * inv[:, h:h + 1]

    rrow = rrow_ref[...]
    cos2, sin2 = rope_ref[0:1, :], rope_ref[1:2, :]
    qr = _rope_rows(rrow[:, 0:R_DK], cos2, sin2)
    kr = _rope_rows(rrow[:, R_DK:2 * R_DK], cos2, sin2) * RK_SCALE
    qk = jnp.sum(qr * kr, axis=-1, keepdims=True)
    kc = kcol_ref[...]
    kc = (kc * ropec_ref[:, 0:1] + jnp.concatenate([kc[64:], kc[:64]], axis=0) * ropec_ref[:, 1:2]) * RK_SCALE
    qr_bf = jnp.concatenate([qr, jnp.zeros_like(qr)], axis=0).astype(BF16)
    row8 = lax.broadcasted_iota(jnp.int32, (8, R_DV), 0)
    for h in range(R_HEADS):
        gamma = math.exp(LOG_G[h])
        state = st_ref[h]
        v = rrow[h:h + 1, 2 * R_DK:2 * R_DK + R_DV]
        q_state = jnp.sum(jnp.where(row8 == h, _dot(qr_bf, state.astype(BF16)), 0.0), axis=0, keepdims=True)
        out = qk[h:h + 1, :] * v + q_state * gamma
        st_out_ref[h] = state * gamma + kc[:, h:h + 1] * v
        od_ref[:, h * R_DV:(h + 1) * R_DV] = _readout(out, rrow[h:h + 1, 2 * R_DK + R_DV:2 * R_DK + 2 * R_DV],
                                                      gn_ref[:, h * R_DV:(h + 1) * R_DV])


def _odd_sample(c_buf, qkv, rrow, kcol, state, bdil, rel_bias, rope_row, rope_col, gn):
    views = [c_buf.reshape(DEC_BATCH, PAST_LEN // dil, dil * 16, HD) for _, dil in DIL_CFG]
    c_specs = [pl.BlockSpec((None, QB, 16, HD), functools.partial(lambda nb, s: (s, nb, 0, 0), PAST_LEN // dil // QB - 1))
               for _, dil in DIL_CFG]
    return pl.pallas_call(
        _odd_sample_kernel,
        grid=(DEC_BATCH,),
        in_specs=c_specs + [
            pl.BlockSpec((None, 8, 3 * HD), lambda s: (s, 0, 0)),
            pl.BlockSpec((None, R_HEADS, 2 * R_DK + 2 * R_DV), lambda s: (s, 0, 0)),
            pl.BlockSpec((None, R_DK, R_HEADS), lambda s: (s, 0, 0)),
            pl.BlockSpec((None, R_HEADS, R_DK, R_DV), lambda s: (s, 0, 0, 0)),
            pl.BlockSpec((3, QB, 8), lambda s: (0, 0, 0)),
            pl.BlockSpec((N_BUCKETS, 16), lambda s: (0, 0)),
            pl.BlockSpec((2, R_DK), lambda s: (0, 0)),
            pl.BlockSpec((R_DK, 2), lambda s: (0, 0)),
            pl.BlockSpec((1, 1024), lambda s: (0, 0)),
        ],
        out_specs=[pl.BlockSpec((None, 1, 1024), lambda s: (s, 0, 0)),
                   pl.BlockSpec((None, 1, 1024), lambda s: (s, 0, 0)),
                   pl.BlockSpec((None, R_HEADS, R_DK, R_DV), lambda s: (s, 0, 0, 0))],
        out_shape=[jax.ShapeDtypeStruct((DEC_BATCH, 1, 1024), F32),
                   jax.ShapeDtypeStruct((DEC_BATCH, 1, 1024), F32),
                   jax.ShapeDtypeStruct((DEC_BATCH, R_HEADS, R_DK, R_DV), F32)],
        compiler_params=_cparams(("parallel",)),
        name="odd_sample",
    )(*views, qkv, rrow, kcol, state, bdil, rel_bias, rope_row, rope_col, gn.reshape(1, 1024))


def _rope_tables(pos):
    half = R_DK // 2
    inv = 10000.0 ** (-jnp.arange(half, dtype=F32) / half)
    ang = pos.astype(F32)[:, None] * inv[None, :]
    cos, sin = jnp.cos(ang), jnp.sin(ang)
    return jnp.concatenate([cos, cos], axis=-1), jnp.concatenate([-sin, sin], axis=-1)


def _permute_even_w(w):
    qa, kva, ga, qb, kvb, qi, ki, wi = jnp.split(w, [1024, 2560, 2584, 3608, 4120, 4376, 4440], axis=1)
    pad = jnp.zeros((w.shape[0], EVEN_W - w.shape[1]), w.dtype)
    return jnp.concatenate([qa, qb, kva, kvb, qi, ki, wi, ga, pad], axis=1)


def _even_decode(zs, e, bias, page_table, cache_a_cmp, cache_a_sel, cache_b_kv, cache_b_idx, state_a_win,
                 cmp_pe, cmp_w_bf):
    dec, decc = bias[2], bias[3]
    n_pool = cache_a_cmp.shape[1]
    cmp_s = _even_first(_compress_sample(cache_a_cmp[e].reshape(n_pool, 4 * PAGE, HD), page_table, cmp_pe, cmp_w_bf),
                        DEC_BATCH)
    heads = lambda a: a.reshape(DEC_BATCH, 8, HD)
    rep = lambda a: jnp.repeat(a.reshape(DEC_BATCH, 2, HD), 4, axis=1)
    kva = zs[:, E_CMP:E_KVB].reshape(DEC_BATCH, 3, 2, 2, HD)
    kvb = zs[:, E_KVB:E_QI].reshape(DEC_BATCH, 2, 2, HD)
    sm = zs[:, E_SM:E_SM + 128]
    zero4 = lambda a: jnp.concatenate([a, jnp.zeros_like(a)], axis=1)
    qi8 = zero4(zs[:, E_QI:E_SM].reshape(DEC_BATCH, 4, 64))
    ki8 = jnp.broadcast_to(sm[:, None, SM_KI:SM_KI + 64], (DEC_BATCH, 8, 64))
    gates = sm[:, SM_GA:SM_GA + 24].reshape(DEC_BATCH, 8, 3)
    wi8 = zero4(sm[:, SM_WI:SM_WI + 4].reshape(DEC_BATCH, 4, 1))
    misc = jnp.concatenate([gates, wi8, jnp.zeros((DEC_BATCH, 8, DEC_IN_W - DI_MISC - 4), F32)], axis=2)
    din = jnp.concatenate([
        heads(zs[:, E_QA:E_QB]), heads(zs[:, E_QB:E_CMP]),
        rep(kva[:, 1, :, 0]), rep(kva[:, 1, :, 1]), rep(kva[:, 2, :, 0]), rep(kva[:, 2, :, 1]),
        rep(kvb[:, :, 0]), rep(kvb[:, :, 1]), qi8, ki8, misc], axis=2)
    expand = (jnp.arange(2 * PAST_LEN)[None, :] // (2 * SEL_BLOCK) == jnp.arange(N_CMP)[:, None]).astype(BF16)
    o_s = _even_sample(page_table, cache_a_sel[e].reshape(n_pool, 4 * PAGE, HD),
                       cache_b_kv[e].reshape(n_pool, 4 * PAGE, HD), cache_b_idx[e],
                       din, cmp_s, state_a_win[e].reshape(DEC_BATCH, 4 * A_WINDOW, HD), expand, dec, decc)
    return o_s[:, :, :HD].reshape(DEC_BATCH, 1024), o_s[:, :, HD:].reshape(DEC_BATCH, 1024)


def _even_layer(x, e, bias, page_table, cache_a_cmp, cache_a_sel, cache_b_kv, cache_b_idx, state_a_win,
                g, w_in, w_out, cmp_pe, cmp_w):
    bnear, bcmp = bias[0], bias[1]
    z, zb = _proj_in(x, g, _permute_even_w(w_in).astype(BF16), 896)
    cmp_w_bf = cmp_w.astype(BF16)
    cmp_p = _even_first(_compress_prompt(z, cmp_pe, cmp_w_bf), BATCH)
    oa_p = _nsa_prompt(z, zb, cmp_p, bnear[:, 0:8], bcmp)
    ob_p = _dsa_prompt(z, zb, bnear[:, 8:16])
    zs = z[N_TOK_P:]
    oa_s, ob_s = _even_decode(zs, e, bias, page_table, cache_a_cmp, cache_a_sel, cache_b_kv, cache_b_idx,
                              state_a_win, cmp_pe, cmp_w_bf)
    oa = jnp.concatenate([oa_p, oa_s], axis=0)
    ob = jnp.concatenate([ob_p, ob_s], axis=0)
    x = _proj_out(x, oa, ob, w_out[:1024].astype(BF16), w_out[1024:].astype(BF16))

    zp = z[:N_TOK_P].reshape(BATCH, SEQ, EVEN_W)
    kv5 = lambda a, n: a.reshape(n, -1, 2, 2, HD)
    state = (
        kv5(zp[:, :, E_CMP:E_SEL], BATCH), kv5(zs[:, E_CMP:E_SEL], DEC_BATCH),
        kv5(zp[:, :, E_SEL:E_WIN], BATCH), kv5(zs[:, E_SEL:E_WIN], DEC_BATCH),
        kv5(zp[:, :, E_KVB:E_QI], BATCH), kv5(zs[:, E_KVB:E_QI], DEC_BATCH),
        zp[:, :, E_SM:E_SM + 64], zs[:, None, E_SM:E_SM + 64],
        kv5(zp[:, SEQ - A_WINDOW:, E_WIN:E_KVB], BATCH), kv5(zs[:, E_WIN:E_KVB], DEC_BATCH),
    )
    return x, state


def _odd_layer(x, o, bias, state_c_win, state_ret, g, w_in, w_out, gn):
    bnear, bdil = bias[0], bias[4]
    z, zb = _proj_in(x, g, w_in.astype(BF16), 1536)
    cos_p, sin_p = _rope_tables(jnp.arange(SEQ))
    oc_p = _dil_prompt(z, zb, bnear[:, 0:8])
    od_p, ret_p = _ret_prompt(z, cos_p, sin_p, gn)

    zs = z[N_TOK_P:]
    kvc = zs[:, O_KVC:O_QR].reshape(DEC_BATCH, 8, 2, HD)
    qkv = jnp.concatenate([zs[:, O_QC:O_KVC].reshape(DEC_BATCH, 8, HD), kvc[:, :, 0], kvc[:, :, 1]], axis=2)
    rrow = jnp.concatenate([zs[:, O_QR:O_KR].reshape(DEC_BATCH, R_HEADS, R_DK),
                            zs[:, O_KR:O_VR].reshape(DEC_BATCH, R_HEADS, R_DK),
                            zs[:, O_VR:O_GR].reshape(DEC_BATCH, R_HEADS, R_DV),
                            zs[:, O_GR:].reshape(DEC_BATCH, R_HEADS, R_DV)], axis=2)
    kcol = jnp.swapaxes(zs[:, O_KR:O_VR].reshape(DEC_BATCH, R_HEADS, R_DK), 1, 2)
    cos_s, sin_s = _rope_tables(jnp.full((1,), PAST_LEN))
    rope_row = jnp.concatenate([cos_s, sin_s], axis=0)
    oc_s, od_s, ret_s = _odd_sample(state_c_win[o], qkv, rrow, kcol,
                                    state_ret[o], bdil, bias[5], rope_row, rope_row.T, gn)
    oc = jnp.concatenate([oc_p, oc_s.reshape(DEC_BATCH, 1024)], axis=0)
    od = jnp.concatenate([od_p, od_s.reshape(DEC_BATCH, 1024)], axis=0)
    x = _proj_out(x, oc, od, w_out[:1024].astype(BF16), w_out[1024:].astype(BF16))
    state = (z[:N_TOK_P, O_KVC:O_QR].reshape(BATCH, SEQ, 8, 2, HD), kvc[:, None], ret_p, ret_s)
    return x, state


def kernel(x_prompt, x_sample, cache_a_cmp, cache_a_sel, cache_b_kv, cache_b_idx, state_a_win, state_c_win,
           state_ret, page_table, p_prompt, p_sample, rel_bias, norm_g, final_norm, ffn_w_in, ffn_w_out,
           ple_gate, ple_proj, even_w_in, even_w_out, nsa_cmp_pe, nsa_cmp_w, odd_w_in, odd_w_out, ret_gn):
    depth = norm_g.shape[0]
    x = jnp.concatenate([x_prompt.reshape(N_TOK_P, D_MODEL), x_sample.reshape(DEC_BATCH, D_MODEL)], axis=0)
    bias = tuple(_bias_tables(rel_bias)) + (rel_bias,)
    w_in_bf = ffn_w_in.astype(BF16)
    w_out_bf = ffn_w_out.astype(BF16)
    even_states, odd_states = [], []
    for i in range(depth):
        x = _ffn_half(x, norm_g[i, 0], w_in_bf[i, 0], w_out_bf[i, 0])
        if i % 2 == 0:
            e = i // 2
            x, st = _even_layer(x, e, bias, page_table, cache_a_cmp, cache_a_sel, cache_b_kv, cache_b_idx,
                                state_a_win, norm_g[i, 1], even_w_in[e], even_w_out[e], nsa_cmp_pe[e], nsa_cmp_w[e])
            even_states.append(st)
        else:
            o = i // 2
            x, st = _odd_layer(x, o, bias, state_c_win, state_ret, norm_g[i, 1], odd_w_in[o], odd_w_out[o],
                               ret_gn[o])
            odd_states.append(st)
        x = _ffn_half(x, norm_g[i, 2], w_in_bf[i, 1], w_out_bf[i, 1])
        p = jnp.concatenate([p_prompt[i].reshape(N_TOK_P, PLE_DIM), p_sample[i].reshape(DEC_BATCH, PLE_DIM)], axis=0)
        x = _ple(x, p, norm_g[i, 3], ple_gate[i].astype(BF16), ple_proj[i].astype(BF16), final_norm,
                 final=(i == depth - 1))
    y_prompt = x[:N_TOK_P].reshape(BATCH, SEQ, D_MODEL)
    y_sample = x[N_TOK_P:].reshape(DEC_BATCH, 1, D_MODEL)
    ev = [jnp.stack([st[k] for st in even_states]) for k in range(10)]
    od = [jnp.stack([st[k] for st in odd_states]) for k in range(4)]
    return (y_prompt, y_sample, *ev, *od)
```

```python
import functools
import math

import numpy as np
import jax
import jax.numpy as jnp
from jax import lax
from jax.experimental import pallas as pl
from jax.experimental.pallas import tpu as pltpu

F32 = jnp.float32
BF16 = jnp.bfloat16
HI = lax.Precision.HIGHEST

D_MODEL = 2048
BATCH = 4
SEQ = 2048
DEC_BATCH = 128
PAST_LEN = 2048
PAGE = 128
N_PAGES = PAST_LEN // PAGE
HD = 128
D_FF = 5632
PLE_DIM = 256
N_TOK_P = BATCH * SEQ
N_TOK = N_TOK_P + DEC_BATCH
QB = 128
NQB = SEQ // QB
CMP_BLOCK = 32
N_CMP = SEQ // CMP_BLOCK
SEL_BLOCK = 64
N_SEL = 16
A_WINDOW = 512
IDX_TOPK = 256
R_HEADS = 4
R_DK = 128
R_DV = 256
N_BUCKETS = 32
RMS_EPS = 1e-6
NEG_INF = -1e30
BIG = 3.0e38
FORCE_SCORE = 1e4
Q_SCALE = HD ** -0.5
LOG2E = math.log2(math.e)
RK_SCALE = R_DK ** -0.5
EVEN_W = 4480
ODD_W = 6144
VMEM_LIMIT = 56 * 1024 * 1024
BISECT_STEPS = 10

E_QA, E_QB, E_CMP, E_SEL, E_WIN, E_KVB, E_QI, E_SM = 0, 1024, 2048, 2560, 3072, 3584, 4096, 4352
SM_KI, SM_WI, SM_GA = 0, 64, 68
O_QC, O_KVC, O_QR, O_KR, O_VR, O_GR = 0, 1024, 3072, 3584, 4096, 5120

LOG_G = [math.log1p(-(2.0 ** (-5.0 - h))) for h in range(R_HEADS)]


def _cparams(sem=None):
    return pltpu.CompilerParams(dimension_semantics=sem, vmem_limit_bytes=VMEM_LIMIT)


def _dot(a, b, precision=None):
    return jnp.dot(a, b, preferred_element_type=F32, precision=precision)


def _dot_nt(a, b, precision=None):
    return lax.dot_general(a, b, (((1,), (1,)), ((), ())), preferred_element_type=F32, precision=precision)


def _sigmoid(x):
    return 1.0 / (1.0 + jnp.exp(-x))


def _rms(x, g):
    return x * lax.rsqrt(jnp.mean(x * x, axis=-1, keepdims=True) + RMS_EPS) * g


FFN_TM, FFN_TF = 640, 512


def _ffn_kernel(x_ref, g_ref, wg_ref, wu_ref, wo_ref, o_ref, h_scr, acc_scr):
    j = pl.program_id(1)

    @pl.when(j == 0)
    def _():
        h_scr[...] = _rms(x_ref[...], g_ref[...]).astype(BF16)
        acc_scr[...] = jnp.zeros_like(acc_scr)

    h = h_scr[...]
    gate = _dot(h, wg_ref[...])
    up = _dot(h, wu_ref[...])
    act = gate * _sigmoid(gate) * up
    acc_scr[...] += _dot(act.astype(BF16), wo_ref[...])

    @pl.when(j == pl.num_programs(1) - 1)
    def _():
        o_ref[...] = x_ref[...] + 0.5 * acc_scr[...]


def _ffn_half(x, g, w_in, w_out, layer, half):
    nj = D_FF // FFN_TF
    return pl.pallas_call(
        _ffn_kernel,
        grid=(N_TOK // FFN_TM, nj),
        in_specs=[
            pl.BlockSpec((FFN_TM, D_MODEL), lambda i, j: (i, 0)),
            pl.BlockSpec((1, D_MODEL), lambda i, j: (0, 0)),
            pl.BlockSpec((None, None, D_MODEL, FFN_TF), lambda i, j: (layer, half, 0, j)),
            pl.BlockSpec((None, None, D_MODEL, FFN_TF), lambda i, j: (layer, half, 0, j + nj)),
            pl.BlockSpec((None, None, FFN_TF, D_MODEL), lambda i, j: (layer, half, j, 0)),
        ],
        out_specs=pl.BlockSpec((FFN_TM, D_MODEL), lambda i, j: (i, 0)),
        out_shape=jax.ShapeDtypeStruct((N_TOK, D_MODEL), F32),
        scratch_shapes=[pltpu.VMEM((FFN_TM, D_MODEL), BF16), pltpu.VMEM((FFN_TM, D_MODEL), F32)],
        compiler_params=_cparams(("parallel", "arbitrary")),
        name="ffn_half",
    )(x, g.reshape(1, D_MODEL), w_in, w_in, w_out)


PROJ_TM = 640


def _proj_in_kernel(x_ref, g_ref, w_ref, z_ref, zb_ref, h_scr):
    @pl.when(pl.program_id(1) == 0)
    def _():
        h_scr[...] = _rms(x_ref[...], g_ref[...]).astype(BF16)

    z = _dot(h_scr[...], w_ref[...])
    z_ref[...] = z
    zb_ref[...] = z.astype(BF16)


def _proj_in(x, g, w, tn):
    n = w.shape[1]
    return pl.pallas_call(
        _proj_in_kernel,
        grid=(N_TOK // PROJ_TM, n // tn),
        in_specs=[
            pl.BlockSpec((PROJ_TM, D_MODEL), lambda i, j: (i, 0)),
            pl.BlockSpec((1, D_MODEL), lambda i, j: (0, 0)),
            pl.BlockSpec((D_MODEL, tn), lambda i, j: (0, j)),
        ],
        out_specs=[pl.BlockSpec((PROJ_TM, tn), lambda i, j: (i, j)),
                   pl.BlockSpec((PROJ_TM, tn), lambda i, j: (i, j))],
        out_shape=[jax.ShapeDtypeStruct((N_TOK, n), F32), jax.ShapeDtypeStruct((N_TOK, n), BF16)],
        scratch_shapes=[pltpu.VMEM((PROJ_TM, D_MODEL), BF16)],
        compiler_params=_cparams(("parallel", "arbitrary")),
        name="proj_in",
    )(x, g.reshape(1, D_MODEL), w)


OUT_TM = 320


def _proj_out_kernel(x_ref, a_ref, b_ref, wa_ref, wb_ref, o_ref):
    o_ref[...] = (x_ref[...] + _dot(a_ref[...].astype(BF16), wa_ref[...])
                  + _dot(b_ref[...].astype(BF16), wb_ref[...]))


def _proj_out(x, a, b, wa, wb):
    ka, kb = a.shape[1], b.shape[1]
    return pl.pallas_call(
        _proj_out_kernel,
        grid=(N_TOK // OUT_TM,),
        in_specs=[
            pl.BlockSpec((OUT_TM, D_MODEL), lambda i: (i, 0)),
            pl.BlockSpec((OUT_TM, ka), lambda i: (i, 0)),
            pl.BlockSpec((OUT_TM, kb), lambda i: (i, 0)),
            pl.BlockSpec((ka, D_MODEL), lambda i: (0, 0)),
            pl.BlockSpec((kb, D_MODEL), lambda i: (0, 0)),
        ],
        out_specs=pl.BlockSpec((OUT_TM, D_MODEL), lambda i: (i, 0)),
        out_shape=jax.ShapeDtypeStruct((N_TOK, D_MODEL), F32),
        compiler_params=_cparams(("parallel",)),
        name="proj_out",
    )(x, a, b, wa, wb)


def _ple_kernel(x_ref, p_ref, g_ref, wg_ref, wp_ref, gf_ref, o_ref, *, final):
    x = x_ref[...]
    gate = _sigmoid(_dot(_rms(x, g_ref[...]).astype(BF16), wg_ref[...]))
    y = x + gate * _dot(p_ref[...].astype(BF16), wp_ref[...])
    o_ref[...] = _rms(y, gf_ref[...]) if final else y


def _ple(x, p, g, w_gate, w_proj, g_final, final):
    return pl.pallas_call(
        functools.partial(_ple_kernel, final=final),
        grid=(N_TOK // OUT_TM,),
        in_specs=[
            pl.BlockSpec((OUT_TM, D_MODEL), lambda i: (i, 0)),
            pl.BlockSpec((OUT_TM, PLE_DIM), lambda i: (i, 0)),
            pl.BlockSpec((1, D_MODEL), lambda i: (0, 0)),
            pl.BlockSpec((D_MODEL, D_MODEL), lambda i: (0, 0)),
            pl.BlockSpec((PLE_DIM, D_MODEL), lambda i: (0, 0)),
            pl.BlockSpec((1, D_MODEL), lambda i: (0, 0)),
        ],
        out_specs=pl.BlockSpec((OUT_TM, D_MODEL), lambda i: (i, 0)),
        out_shape=jax.ShapeDtypeStruct((N_TOK, D_MODEL), F32),
        compiler_params=_cparams(("parallel",)),
        name="ple_add",
    )(x, p, g.reshape(1, D_MODEL), w_gate, w_proj, g_final.reshape(1, D_MODEL))


def _t5_bucket_np(dist):
    dist = np.maximum(np.asarray(dist, np.int64), 0)
    exact = N_BUCKETS // 2
    out = {}
    for dt in (np.float32, np.float64):
        scaled = np.log(np.maximum(dist, exact).astype(dt) / dt(exact)) / dt(math.log(128 / exact))
        large = np.minimum(exact + (scaled * dt(N_BUCKETS - exact)).astype(np.int64), N_BUCKETS - 1)
        out[dt] = np.where(dist < exact, dist, large)
    assert np.array_equal(out[np.float32], out[np.float64])
    return out[np.float32].astype(np.int32)


def _cmp_block_of_col(col):
    return 2 * (col % 32) + col // 32


def _bias_kernel(tbl_ref, bk_near, bk_cmp, bk_dec, bk_decc, bk_dil, o_near, o_cmp, o_dec, o_decc, o_dil):
    def lookup(bucket, h):
        acc = jnp.zeros(bucket.shape, F32)
        for b in range(N_BUCKETS):
            acc = jnp.where(bucket == b, tbl_ref[b, h], acc)
        return acc

    for rel in range(3):
        bk = bk_near[rel]
        for h in range(16):
            o_near[rel, h] = lookup(bk, h) * LOG2E
    bk = bk_cmp[...]
    for h in range(8):
        o_cmp[h] = lookup(bk, h)
    bk = bk_dec[...]
    for h in range(16):
        o_dec[h:h + 1, :] = lookup(bk, h)
    bk = bk_decc[...]
    for h in range(8):
        o_decc[h:h + 1, :] = lookup(bk, h)
    lane = lax.broadcasted_iota(jnp.int32, (QB, 8), 1)
    for cfg in range(3):
        bk = bk_dil[cfg]
        acc = jnp.zeros((QB, 8), F32)
        for h in range(8):
            acc = jnp.where(lane == h, lookup(bk, h), acc)
        o_dil[cfg] = acc


DEC_W = 2 * 2176


def _bias_tables(rel_bias):
    r = np.arange(QB)[:, None]
    c = np.arange(QB)[None, :]
    bk_near = np.stack([_t5_bucket_np(rel * QB + r - c) for rel in range(3)])
    t = np.arange(SEQ)[:, None]
    col = np.arange(N_CMP)[None, :]
    bk_cmp = _t5_bucket_np(t - (CMP_BLOCK * _cmp_block_of_col(col) + CMP_BLOCK - 1))
    bk_dec = _t5_bucket_np(PAST_LEN - np.arange(DEC_W)[None, :] // 2)
    bk_decc = _t5_bucket_np(PAST_LEN - (CMP_BLOCK * _cmp_block_of_col(col) + CMP_BLOCK - 1))
    i = np.arange(QB)[:, None]
    bk_dil = np.stack([np.broadcast_to(_t5_bucket_np(dil * (QB - i)), (QB, 8)) for dil in (1, 4, 16)])
    vm = pl.BlockSpec(memory_space=pltpu.VMEM)
    return pl.pallas_call(
        _bias_kernel,
        in_specs=[pl.BlockSpec(memory_space=pltpu.SMEM), vm, vm, vm, vm, vm],
        out_specs=[vm] * 5,
        out_shape=[
            jax.ShapeDtypeStruct((3, 16, QB, QB), F32),
            jax.ShapeDtypeStruct((8, SEQ, N_CMP), F32),
            jax.ShapeDtypeStruct((16, DEC_W), F32),
            jax.ShapeDtypeStruct((8, N_CMP), F32),
            jax.ShapeDtypeStruct((3, QB, 8), F32),
        ],
        compiler_params=_cparams(),
        name="bias_tables",
    )(rel_bias, jnp.asarray(bk_near), jnp.asarray(bk_cmp), jnp.asarray(bk_dec), jnp.asarray(bk_decc),
      jnp.asarray(bk_dil))


def _compress_rows(rows_ref, pe_ref, w_ref, out_ref, nblk):
    for c in range(2):
        acc = jnp.zeros((2 * nblk, HD), F32)
        for l in range(CMP_BLOCK):
            pe = pe_ref[c, l:l + 1, :]
            xs = [rows_ref[pl.ds(4 * l + 2 * g + c, nblk, stride=4 * CMP_BLOCK), :] + pe for g in range(2)]
            acc = acc + _dot(jnp.concatenate(xs, axis=0).astype(BF16), w_ref[c, l])
        for g in range(2):
            out_ref[:, (2 * g + c) * HD:(2 * g + c + 1) * HD] = acc[g * nblk:(g + 1) * nblk]


CMP_ROWS_P = 4096


def _compress_prompt_kernel(rows_ref, pe_ref, w_ref, out_ref):
    _compress_rows(rows_ref, pe_ref, w_ref, out_ref, CMP_ROWS_P // CMP_BLOCK)


def _compress_prompt(z, pe, w):
    nblk = CMP_ROWS_P // CMP_BLOCK
    rows = z[:N_TOK_P, E_CMP:E_SEL].reshape(4 * N_TOK_P, HD)
    return pl.pallas_call(
        _compress_prompt_kernel,
        grid=(N_TOK_P // CMP_ROWS_P,),
        in_specs=[
            pl.BlockSpec((4 * CMP_ROWS_P, HD), lambda i: (i, 0)),
            pl.BlockSpec((2, CMP_BLOCK, HD), lambda i: (0, 0, 0)),
            pl.BlockSpec((2, CMP_BLOCK, HD, HD), lambda i: (0, 0, 0, 0)),
        ],
        out_specs=pl.BlockSpec((nblk, 512), lambda i: (i, 0)),
        out_shape=jax.ShapeDtypeStruct((N_TOK_P // CMP_BLOCK, 512), F32),
        compiler_params=_cparams(("parallel",)),
        name="nsa_compress_prompt",
    )(rows, pe, w)


CMP_SEQS = 2


def _compress_sample_kernel(pt_ref, *refs):
    pages = refs[:CMP_SEQS * N_PAGES]
    pe_ref, w_ref, out_ref, rows_scr = refs[CMP_SEQS * N_PAGES:]
    for k, pg in enumerate(pages):
        rows_scr[4 * k * PAGE:4 * (k + 1) * PAGE, :] = pg[...]
    _compress_rows(rows_scr, pe_ref, w_ref, out_ref, CMP_SEQS * N_CMP)


def _page_map(s_local, k, n_seqs, i, pt_ref):
    return (pt_ref[i * n_seqs + s_local, k], 0, 0)


def _compress_sample(pool, page_table, pe, w):
    page_specs = [pl.BlockSpec((None, 4 * PAGE, HD), functools.partial(_page_map, s, k, CMP_SEQS))
                  for s in range(CMP_SEQS) for k in range(N_PAGES)]
    nblk = CMP_SEQS * N_CMP
    grid_spec = pltpu.PrefetchScalarGridSpec(
        num_scalar_prefetch=1,
        grid=(DEC_BATCH // CMP_SEQS,),
        in_specs=page_specs + [
            pl.BlockSpec((2, CMP_BLOCK, HD), lambda i, pt: (0, 0, 0)),
            pl.BlockSpec((2, CMP_BLOCK, HD, HD), lambda i, pt: (0, 0, 0, 0)),
        ],
        out_specs=pl.BlockSpec((nblk, 512), lambda i, pt: (i, 0)),
        scratch_shapes=[pltpu.VMEM((4 * CMP_SEQS * PAST_LEN, HD), F32)],
    )
    return pl.pallas_call(
        _compress_sample_kernel,
        grid_spec=grid_spec,
        out_shape=jax.ShapeDtypeStruct((DEC_BATCH * N_CMP, 512), F32),
        compiler_params=_cparams(("parallel",)),
        name="nsa_compress_sample",
    )(page_table, *([pool] * (CMP_SEQS * N_PAGES)), pe, w)


def _even_first(cmp_sum, n_seq):
    x = cmp_sum.reshape(n_seq, N_CMP // 2, 2, 512)
    return jnp.swapaxes(x, 1, 2).reshape(n_seq, N_CMP, 512)


RG = 512
N_RG = SEQ // RG
RG_CHUNKS = RG // QB


def _attend_regions(i, streams, st, slot0=0, min_chunk=None, qk_first=False):
    m_scr, acc_scr = st
    ones = jnp.ones((RG, HD), BF16)
    for j in range(len(streams)):
        m_scr[slot0 + j] = jnp.full(m_scr.shape[1:], NEG_INF, F32)
        acc_scr[slot0 + j] = jnp.zeros(acc_scr.shape[1:], F32)
    for rg in range(N_RG):
        visit = rg * RG_CHUNKS <= i
        if min_chunk is not None:
            visit = visit & ((rg + 1) * RG_CHUNKS - 1 >= min_chunk)

        @pl.when(visit)
        def _():
            rows = slice(rg * RG, (rg + 1) * RG)
            qk = lambda q_bf, kv_ref, kcol: _dot_nt(q_bf, kv_ref[rows, kcol:kcol + HD])

            def pv(slot, p_bf, alpha, kv_ref, vcol):
                v1 = jnp.concatenate([kv_ref[rows, vcol:vcol + HD], ones], axis=1)
                acc_scr[slot] = alpha * acc_scr[slot] + _dot(p_bf, v1)

            raw = [qk(*stream[:3]) for stream in streams] if qk_first else None
            probs = []
            for j, (q_bf, kv_ref, kcol, vcol, logit_fn) in enumerate(streams):
                slot = slot0 + j
                s = logit_fn(rg, raw[j] if qk_first else qk(q_bf, kv_ref, kcol))
                m_old = m_scr[slot]
                m_new = jnp.maximum(m_old, jnp.max(s, axis=-1, keepdims=True))
                alpha = jnp.exp2(m_old - m_new)
                p_bf = jnp.exp2(s - m_new).astype(BF16)
                m_scr[slot] = m_new
                if qk_first:
                    probs.append((slot, p_bf, alpha, kv_ref, vcol))
                else:
                    pv(slot, p_bf, alpha, kv_ref, vcol)
            for args in probs:
                pv(*args)

    outs = []
    for j in range(len(streams)):
        acc = acc_scr[slot0 + j]
        outs.append(acc[:, :HD] / jnp.maximum(acc[:, HD:], 1e-30))
    return outs


def _softmax_state(slots, rows):
    return [pltpu.VMEM((slots, rows, 1), F32), pltpu.VMEM((slots, rows, 2 * HD), F32)]


def _mask_rows4(ok, logits):
    w = logits.shape[1]
    return jnp.where(ok[None], logits.reshape(4, QB, w), NEG_INF).reshape(4 * QB, w)


def _stack_heads(q_ref, g):
    return jnp.concatenate([q_ref[:, (4 * g + a) * HD:(4 * g + a + 1) * HD] for a in range(4)], axis=0)


def _rel_dist(i, kb):
    r = lax.broadcasted_iota(jnp.int32, (QB, QB), 0)
    c = lax.broadcasted_iota(jnp.int32, (QB, QB), 1)
    return (i - kb) * QB + r - c


def _region_dist(i, rg):
    r = lax.broadcasted_iota(jnp.int32, (QB, RG), 0)
    c = lax.broadcasted_iota(jnp.int32, (QB, RG), 1)
    return i * QB - rg * RG + r - c


def _region_bias(bnear_ref, i, rg, heads):
    tiles = [bnear_ref[jnp.clip(i - (rg * RG_CHUNKS + c), 0, 2), heads].reshape(-1, QB) for c in range(RG_CHUNKS)]
    return jnp.concatenate(tiles, axis=1)


def _split_bf16(a):
    hi = a.astype(BF16)
    return hi, (a - hi.astype(F32)).astype(BF16)


def _dot_nt3(a, b):
    ah, al = _split_bf16(a)
    bh, bl = _split_bf16(b)
    return _dot_nt(ah, bh) + (_dot_nt(ah, bl) + _dot_nt(al, bh))


def _rank_select(score, n_blocks, n_take):
    blk = lax.broadcasted_iota(jnp.int32, score.shape, 1)
    rank = jnp.zeros(score.shape, F32)
    for j in range(n_blocks):
        col = score[:, j:j + 1]
        beats = (col > score) | ((col == score) & (blk > j))
        rank = rank + jnp.where(beats, 1.0, 0.0)
    return jnp.where(rank < n_take, 1.0, 0.0)


def _nsa_prompt_kernel(q_ref, sm_ref, ksel_ref, kwin_ref, cmp_ref, bnear_ref, bcmp_ref, o_ref,
                       selk_scr, m_scr, acc_scr):
    i = pl.program_id(1)
    st = (m_scr, acc_scr)
    sm = sm_ref[...]
    jj = lax.broadcasted_iota(jnp.int32, (N_CMP // 2, SEQ), 0)
    cc = lax.broadcasted_iota(jnp.int32, (N_CMP // 2, SEQ), 1)
    expand = jnp.where(jj == cc // SEL_BLOCK, 1.0, 0.0).astype(BF16)
    row = lax.broadcasted_iota(jnp.int32, (4 * QB, N_CMP), 0) & (QB - 1)
    col = lax.broadcasted_iota(jnp.int32, (4 * QB, N_CMP), 1)
    dist_c = i * QB + row - (CMP_BLOCK * _cmp_block_of_col(col) + CMP_BLOCK - 1)
    ok_c = dist_c >= 0
    t_q = i * QB + lax.broadcasted_iota(jnp.int32, (QB, N_CMP // 2), 0)
    blk = lax.broadcasted_iota(jnp.int32, (QB, N_CMP // 2), 1)
    cur = t_q // SEL_BLOCK
    forced = (blk == 0) | (blk == cur) | (blk == cur - 1)
    admissible = blk * SEL_BLOCK <= t_q
    q_bf, o_c = [], []
    for g in range(2):
        q = _stack_heads(q_ref, g) * Q_SCALE
        q_bf.append((q * LOG2E).astype(BF16))
        logit = _dot_nt3(q, cmp_ref[:, 2 * g * HD:(2 * g + 1) * HD]) + bcmp_ref[4 * g:4 * g + 4].reshape(4 * QB, N_CMP)
        logit = jnp.where(ok_c, logit, NEG_INF)
        e = jnp.where(ok_c, jnp.exp(logit - jnp.max(logit, axis=-1, keepdims=True)), 0.0)
        p_c = e / jnp.maximum(jnp.sum(e, axis=-1, keepdims=True), 1e-30)
        o_c.append(_dot(p_c.astype(BF16), cmp_ref[:, (2 * g + 1) * HD:(2 * g + 2) * HD].astype(BF16)))
        p_g = p_c[0:QB] + p_c[QB:2 * QB] + p_c[2 * QB:3 * QB] + p_c[3 * QB:4 * QB]
        score = p_g[:, :N_CMP // 2] + p_g[:, N_CMP // 2:]
        score = jnp.where(forced, FORCE_SCORE, score)
        score = jnp.where(admissible, score, NEG_INF)
        sel = _rank_select(score, SEQ // SEL_BLOCK, N_SEL).astype(BF16)
        selk_scr[g] = _dot(sel, expand)

    def sel_logits(g, rg, raw):
        ok = (selk_scr[g, :, rg * RG:(rg + 1) * RG] > 0.5) & (_region_dist(i, rg) >= 0)
        return _mask_rows4(ok, raw + _region_bias(bnear_ref, i, rg, slice(4 * g, 4 * g + 4)))

    def win_logits(g, rg, raw):
        d = _region_dist(i, rg)
        return _mask_rows4((d >= 0) & (d <= A_WINDOW), raw + _region_bias(bnear_ref, i, rg, slice(4 * g, 4 * g + 4)))

    o_s = _attend_regions(i, [(q_bf[g], ksel_ref, 2 * g * HD, (2 * g + 1) * HD, functools.partial(sel_logits, g))
                              for g in range(2)], st, slot0=0)
    o_w = _attend_regions(i, [(q_bf[g], kwin_ref, 2 * g * HD, (2 * g + 1) * HD, functools.partial(win_logits, g))
                              for g in range(2)], st, slot0=2, min_chunk=i - A_WINDOW // QB)
    for g in range(2):
        gates = []
        for br in range(3):
            gates.append(jnp.concatenate(
                [_sigmoid(sm[:, SM_GA + 3 * (4 * g + a) + br:SM_GA + 3 * (4 * g + a) + br + 1]) for a in range(4)],
                axis=0))
        out = gates[0] * o_c[g] + gates[1] * o_s[g] + gates[2] * o_w[g]
        for a in range(4):
            o_ref[:, (4 * g + a) * HD:(4 * g + a + 1) * HD] = out[a * QB:(a + 1) * QB]


def _nsa_prompt(z, zb, cmp_sum, bnear, bcmp):
    return pl.pallas_call(
        _nsa_prompt_kernel,
        grid=(BATCH, NQB),
        in_specs=[
            pl.BlockSpec((QB, 1024), lambda b, i: (b * NQB + i, E_QA // 1024)),
            pl.BlockSpec((QB, 128), lambda b, i: (b * NQB + i, E_SM // 128)),
            pl.BlockSpec((SEQ, 512), lambda b, i: (b, E_SEL // 512)),
            pl.BlockSpec((SEQ, 512), lambda b, i: (b, E_WIN // 512)),
            pl.BlockSpec((None, N_CMP, 512), lambda b, i: (b, 0, 0)),
            pl.BlockSpec((3, 8, QB, QB), lambda b, i: (0, 0, 0, 0)),
            pl.BlockSpec((8, QB, N_CMP), lambda b, i: (0, i, 0)),
        ],
        out_specs=pl.BlockSpec((QB, 1024), lambda b, i: (b * NQB + i, 0)),
        out_shape=jax.ShapeDtypeStruct((N_TOK_P, 1024), F32),
        scratch_shapes=[pltpu.VMEM((2, QB, SEQ), F32)] + _softmax_state(4, 4 * QB),
        compiler_params=_cparams(("parallel", "arbitrary")),
        name="nsa_prompt",
    )(z, z, zb, zb, cmp_sum, bnear, bcmp)


def _dsa_prompt_kernel(q_ref, qi_ref, sm_ref, kidx_ref, kv_ref, bnear_ref, o_ref,
                       sc_scr, mk_scr, m_scr, acc_scr):
    i = pl.program_id(1)
    sm = sm_ref[...]
    qi = qi_ref[...]
    wi = sm[:, SM_WI:SM_WI + 4] * 0.5

    for rg in range(N_RG):
        cols = slice(rg * RG, (rg + 1) * RG)

        @pl.when(rg * RG_CHUNKS <= i)
        def _():
            kid = kidx_ref[cols, SM_KI:SM_KI + 64]
            sc = jnp.zeros((QB, RG), F32)
            for h in range(4):
                sc = sc + wi[:, h:h + 1] * jnp.maximum(_dot_nt3(qi[:, 64 * h:64 * h + 64], kid), 0.0)
            sc_scr[:, cols] = jnp.where(_region_dist(i, rg) >= 0, sc, NEG_INF)

        @pl.when(rg * RG_CHUNKS > i)
        def _():
            sc_scr[:, cols] = jnp.full((QB, RG), NEG_INF, F32)

    scores = sc_scr[...]
    lo = jnp.min(jnp.where(scores > 0.5 * NEG_INF, scores, BIG), axis=-1, keepdims=True)
    hib = jnp.max(scores, axis=-1, keepdims=True)
    n_causal = (i * QB + 1 + lax.broadcasted_iota(jnp.int32, (QB, 1), 0)).astype(F32)
    k_eff = jnp.minimum(n_causal, float(IDX_TOPK))

    def count(thr, strict):
        s = sc_scr[...]
        return jnp.sum(jnp.where((s > thr) if strict else (s >= thr), 1.0, 0.0), axis=-1, keepdims=True)

    def max_below(bound):
        s = sc_scr[...]
        return jnp.max(jnp.where(s < bound, s, -BIG), axis=-1, keepdims=True)

    hix = jnp.full((QB, 1), BIG, F32)
    for _ in range(BISECT_STEPS):
        mid = 0.5 * (lo + hib)
        ge = count(mid, False) >= k_eff
        lo = jnp.where(ge, mid, lo)
        hib = jnp.where(ge, hib, mid)
        hix = jnp.where(ge, hix, mid)

    def step_cond(c):
        return c[3] > 0.5

    def step_body(c):
        bound, thr, done, _ = c
        cand = max_below(bound)
        ok = count(cand, False) >= k_eff
        active = done < 0.5
        thr = jnp.where(active & ok, cand, thr)
        bound = jnp.where(active & jnp.logical_not(ok), cand, bound)
        done = jnp.where(ok, 1.0, done)
        return bound, thr, done, jnp.sum(1.0 - done)

    _, thr, _, _ = lax.while_loop(step_cond, step_body,
                                  (hix, lo, jnp.zeros((QB, 1), F32), jnp.float32(QB)))
    need = k_eff - count(thr, True)
    upper = jnp.where(lax.broadcasted_iota(jnp.int32, (QB, QB), 0) <= lax.broadcasted_iota(jnp.int32, (QB, QB), 1),
                      1.0, 0.0).astype(BF16)

    carry = jnp.zeros((QB, 1), F32)
    for c in range(NQB):
        s = sc_scr[:, c * QB:(c + 1) * QB]
        eq = s == thr
        eqf = jnp.where(eq, 1.0, 0.0)
        prefix = _dot(eqf.astype(BF16), upper) + carry
        mk_scr[:, c * QB:(c + 1) * QB] = jnp.where((s > thr) | (eq & (prefix <= need)), 1.0, 0.0)
        carry = carry + jnp.sum(eqf, axis=-1, keepdims=True)

    def logits(g, rg, raw):
        bias = _region_bias(bnear_ref, i, rg, slice(4 * g, 4 * g + 4))
        return _mask_rows4(mk_scr[:, rg * RG:(rg + 1) * RG] > 0.5, raw + bias)

    streams = [((_stack_heads(q_ref, g) * Q_SCALE * LOG2E).astype(BF16), kv_ref, 2 * g * HD, (2 * g + 1) * HD,
                functools.partial(logits, g)) for g in range(2)]
    outs = _attend_regions(i, streams, (m_scr, acc_scr))
    for g in range(2):
        for a in range(4):
            o_ref[:, (4 * g + a) * HD:(4 * g + a + 1) * HD] = outs[g][a * QB:(a + 1) * QB]


def _dsa_prompt(z, zb, bnear):
    return pl.pallas_call(
        _dsa_prompt_kernel,
        grid=(BATCH, NQB),
        in_specs=[
            pl.BlockSpec((QB, 1024), lambda b, i: (b * NQB + i, E_QB // 1024)),
            pl.BlockSpec((QB, 256), lambda b, i: (b * NQB + i, E_QI // 256)),
            pl.BlockSpec((QB, 128), lambda b, i: (b * NQB + i, E_SM // 128)),
            pl.BlockSpec((SEQ, 128), lambda b, i: (b, E_SM // 128)),
            pl.BlockSpec((SEQ, 512), lambda b, i: (b, E_KVB // 512)),
            pl.BlockSpec((3, 8, QB, QB), lambda b, i: (0, 0, 0, 0)),
        ],
        out_specs=pl.BlockSpec((QB, 1024), lambda b, i: (b * NQB + i, 0)),
        out_shape=jax.ShapeDtypeStruct((N_TOK_P, 1024), F32),
        scratch_shapes=[pltpu.VMEM((QB, SEQ), F32), pltpu.VMEM((QB, SEQ), F32)] + _softmax_state(2, 4 * QB),
        compiler_params=_cparams(("parallel", "arbitrary")),
        name="dsa_prompt",
    )(z, z, z, z, zb, bnear)


DIL_CFG = ((128, 1), (512, 4), (2048, 16))
DIL_REL = 5


def _dil_count(dist):
    cnt = jnp.zeros(dist.shape, F32)
    for window, dil in DIL_CFG:
        cnt = cnt + jnp.where((dist >= 0) & (dist <= window) & ((dist & (dil - 1)) == 0), 1.0, 0.0)
    return cnt


def _dil_prompt_kernel(q_ref, kva_ref, kvb_ref, bnear_ref, o_ref, tab_scr, m_scr, acc_scr):
    i = pl.program_id(1)

    @pl.when(i == 0)
    def _():
        for rel in range(DIL_REL + 2):
            if rel <= DIL_REL:
                cnt = _dil_count(_rel_dist(rel, 0))
                log_n = jnp.where(cnt > 2.5, math.log2(3.0), jnp.where(cnt > 1.5, 1.0,
                                                                       jnp.where(cnt > 0.5, 0.0, NEG_INF)))
            else:
                log_n = jnp.full((QB, QB), NEG_INF, F32)
            for h in range(8):
                tab_scr[rel, h] = bnear_ref[min(rel, 2), h] + log_n

    def logits(h, rg, raw):
        tiles = []
        for c in range(RG_CHUNKS):
            rel = i - (rg * RG_CHUNKS + c)
            tiles.append(tab_scr[jnp.where(rel < 0, DIL_REL + 1, jnp.minimum(rel, DIL_REL)), h])
        return raw + jnp.concatenate(tiles, axis=1)

    streams = []
    for h in range(8):
        col = (h % 4) * 2 * HD
        q_bf = (q_ref[:, h * HD:(h + 1) * HD] * Q_SCALE * LOG2E).astype(BF16)
        streams.append((q_bf, kva_ref if h < 4 else kvb_ref, col, col + HD, functools.partial(logits, h)))
    outs = _attend_regions(i, streams, (m_scr, acc_scr), qk_first=True)
    for h in range(8):
        o_ref[:, h * HD:(h + 1) * HD] = outs[h]


def _dil_prompt(z, zb, bnear):
    return pl.pallas_call(
        _dil_prompt_kernel,
        grid=(BATCH, NQB),
        in_specs=[
            pl.BlockSpec((QB, 1024), lambda b, i: (b * NQB + i, O_QC // 1024)),
            pl.BlockSpec((SEQ, 1024), lambda b, i: (b, O_KVC // 1024)),
            pl.BlockSpec((SEQ, 1024), lambda b, i: (b, O_KVC // 1024 + 1)),
            pl.BlockSpec((3, 8, QB, QB), lambda b, i: (0, 0, 0, 0)),
        ],
        out_specs=pl.BlockSpec((QB, 1024), lambda b, i: (b * NQB + i, 0)),
        out_shape=jax.ShapeDtypeStruct((N_TOK_P, 1024), F32),
        scratch_shapes=[pltpu.VMEM((DIL_REL + 2, 8, QB, QB), F32)] + _softmax_state(8, QB),
        compiler_params=_cparams(("arbitrary", "arbitrary")),
        name="dilated_prompt",
    )(z, zb, zb, bnear)


def _rope_rows(x, cos2, sin2):
    return x * cos2 + pltpu.roll(x, 64, 1) * sin2


def _readout(o, gate, gn):
    mu = jnp.mean(o, axis=-1, keepdims=True)
    d = o - mu
    var = jnp.mean(d * d, axis=-1, keepdims=True)
    return gate * _sigmoid(gate) * (d * lax.rsqrt(var + RMS_EPS)) * gn


def _ret_prompt_kernel(q_ref, k_ref, v_ref, gr_ref, cos_ref, sin_ref, gn_ref, o_ref, st_ref, st_scr):
    ci = pl.program_id(1)

    @pl.when(ci == 0)
    def _():
        st_scr[...] = jnp.zeros_like(st_scr)

    cos2, sin2 = cos_ref[...], sin_ref[...]
    r = lax.broadcasted_iota(jnp.int32, (QB, QB), 0)
    c = lax.broadcasted_iota(jnp.int32, (QB, QB), 1)
    rel = (r - c).astype(F32)
    pos_col = (lax.broadcasted_iota(jnp.int32, (QB, 1), 0)).astype(F32)
    for h in range(R_HEADS):
        lg = LOG_G[h]
        q = _rope_rows(q_ref[:, h * R_DK:(h + 1) * R_DK], cos2, sin2)
        k = _rope_rows(k_ref[:, h * R_DK:(h + 1) * R_DK], cos2, sin2) * RK_SCALE
        v = v_ref[:, h * R_DV:(h + 1) * R_DV]
        state = st_scr[h]
        decay = jnp.where(r >= c, jnp.exp(lg * jnp.maximum(rel, 0.0)), 0.0)
        q_bf, v_bf = q.astype(BF16), v.astype(BF16)
        scores = _dot_nt(q_bf, k.astype(BF16)) * decay
        out = _dot(scores.astype(BF16), v_bf)
        out = out + _dot(q_bf, state.astype(BF16)) * jnp.exp(lg * (pos_col + 1.0))
        zeta = jnp.exp(lg * (QB - 1.0 - pos_col))
        new_state = state * math.exp(lg * QB) + _dot((k * zeta).T.astype(BF16), v_bf)
        st_scr[h] = new_state
        st_ref[h] = new_state
        o_ref[:, h * R_DV:(h + 1) * R_DV] = _readout(out, gr_ref[:, h * R_DV:(h + 1) * R_DV],
                                                     gn_ref[:, h * R_DV:(h + 1) * R_DV])


def _ret_prompt(z, cos2, sin2, gn):
    return pl.pallas_call(
        _ret_prompt_kernel,
        grid=(BATCH, NQB),
        in_specs=[
            pl.BlockSpec((QB, 512), lambda b, i: (b * NQB + i, O_QR // 512)),
            pl.BlockSpec((QB, 512), lambda b, i: (b * NQB + i, O_KR // 512)),
            pl.BlockSpec((QB, 1024), lambda b, i: (b * NQB + i, O_VR // 1024)),
            pl.BlockSpec((QB, 1024), lambda b, i: (b * NQB + i, O_GR // 1024)),
            pl.BlockSpec((QB, 128), lambda b, i: (i, 0)),
            pl.BlockSpec((QB, 128), lambda b, i: (i, 0)),
            pl.BlockSpec((1, 1024), lambda b, i: (0, 0)),
        ],
        out_specs=[pl.BlockSpec((QB, 1024), lambda b, i: (b * NQB + i, 0)),
                   pl.BlockSpec((None, R_HEADS, R_DK, R_DV), lambda b, i: (b, 0, 0, 0))],
        out_shape=[jax.ShapeDtypeStruct((N_TOK_P, 1024), F32),
                   jax.ShapeDtypeStruct((BATCH, R_HEADS, R_DK, R_DV), F32)],
        scratch_shapes=[pltpu.VMEM((R_HEADS, R_DK, R_DV), F32)],
        compiler_params=_cparams(("parallel", "arbitrary")),
        name="retention_prompt",
    )(z, z, z, z, cos2, sin2, gn.reshape(1, 1024))


DI_QA, DI_QB, DI_KSEL, DI_VSEL, DI_KWIN, DI_VWIN, DI_KB, DI_VB, DI_QI, DI_KI, DI_MISC = (
    0, 128, 256, 384, 512, 640, 768, 896, 1024, 1088, 1152)
DEC_IN_W = 1280
DEC_SEL_BLOCKS = PAST_LEN // SEL_BLOCK + 1
DEC_SPLIT, DEC_ROUNDS = 16, 3


def _kv_rows(ref, start, kv):
    return ref[pl.ds(4 * start + kv, 2 * PAGE, stride=2), :].astype(BF16)


def _decode_attend(q, q_bf, chunks, k_new, v_new, bias_new, new_ok, logit_fn):
    row = lax.broadcasted_iota(jnp.int32, (8, 2 * PAGE), 0)
    col = lax.broadcasted_iota(jnp.int32, (8, 2 * PAGE), 1)
    own = (col & 1) == jnp.where(row < 4, 0, 1)
    logits = []
    for k, (ref, start) in enumerate(chunks):
        logits.append(jnp.where(own, logit_fn(k, _dot_nt(q_bf, _kv_rows(ref, start, 0))), NEG_INF))
    s_new = jnp.sum(q * k_new, axis=-1, keepdims=True) + bias_new
    if new_ok is not None:
        s_new = jnp.where(new_ok, s_new, NEG_INF)
    m = s_new
    for s in logits:
        m = jnp.maximum(m, jnp.max(s, axis=-1, keepdims=True))
    p_new = jnp.exp(s_new - m)
    l = p_new
    acc = p_new * v_new
    for s, (ref, start) in zip(logits, chunks):
        p = jnp.exp(s - m)
        l = l + jnp.sum(p, axis=-1, keepdims=True)
        acc = acc + _dot(p.astype(BF16), _kv_rows(ref, start, 1))
    return acc / jnp.maximum(l, 1e-30)


def _even_sample_kernel(pt_ref, *refs):
    sel_pages = refs[0:N_PAGES]
    kvb_pages = refs[N_PAGES:2 * N_PAGES]
    idx_pages = refs[2 * N_PAGES:3 * N_PAGES]
    din_ref, cmp_ref, win_ref, expand_ref, dec_ref, decc_ref, o_ref = refs[3 * N_PAGES:]
    d = din_ref[...]
    row = lax.broadcasted_iota(jnp.int32, (8, QB), 0)
    row_c = lax.broadcasted_iota(jnp.int32, (8, N_CMP), 0)
    pick_c = lambda a, b: jnp.where(row_c < 4, a, b)
    pick = lambda a, b: jnp.where(row < 4, a, b)
    dec_a, dec_b = dec_ref[0:8, :], dec_ref[8:16, :]

    qa = d[:, DI_QA:DI_QA + HD] * Q_SCALE
    qa_bf = qa.astype(BF16)
    cmp = cmp_ref[...]
    logit = pick_c(_dot_nt3(qa, cmp[:, 0:HD]), _dot_nt3(qa, cmp[:, 2 * HD:3 * HD])) + decc_ref[...]
    e = jnp.exp(logit - jnp.max(logit, axis=-1, keepdims=True))
    p_c = e / jnp.maximum(jnp.sum(e, axis=-1, keepdims=True), 1e-30)
    p_bf = p_c.astype(BF16)
    cmp_bf = cmp.astype(BF16)
    o_c = pick(_dot(p_bf, cmp_bf[:, HD:2 * HD]), _dot(p_bf, cmp_bf[:, 3 * HD:4 * HD]))
    g0 = jnp.sum(jnp.where(row_c < 4, p_c, 0.0), axis=0, keepdims=True)
    g1 = jnp.sum(jnp.where(row_c < 4, 0.0, p_c), axis=0, keepdims=True)
    p_g = pick_c(jnp.broadcast_to(g0, (8, N_CMP)), jnp.broadcast_to(g1, (8, N_CMP)))
    blk = lax.broadcasted_iota(jnp.int32, (8, N_CMP), 1)
    score = jnp.concatenate([p_g[:, :N_CMP // 2] + p_g[:, N_CMP // 2:], jnp.zeros((8, N_CMP // 2), F32)], axis=1)
    cur = PAST_LEN // SEL_BLOCK
    score = jnp.where((blk == 0) | (blk == cur) | (blk == cur - 1), FORCE_SCORE, score)
    score = jnp.where(blk < DEC_SEL_BLOCKS, score, -BIG)
    sel = _rank_select(score, DEC_SEL_BLOCKS, N_SEL) * jnp.where(blk < DEC_SEL_BLOCKS, 1.0, 0.0)
    sel_pos = _dot(sel.astype(BF16), expand_ref[...])
    cols = lambda k: slice(2 * k * PAGE, 2 * (k + 1) * PAGE)
    new_col = slice(2 * PAST_LEN, 2 * PAST_LEN + 1)

    def sel_logits(k, raw):
        return jnp.where(sel_pos[:, cols(k)] > 0.5, raw + dec_a[:, cols(k)], NEG_INF)

    o_s = _decode_attend(qa, qa_bf, [(pg, 0) for pg in sel_pages], d[:, DI_KSEL:DI_KSEL + HD],
                         d[:, DI_VSEL:DI_VSEL + HD], dec_a[:, new_col], None, sel_logits)
    win0 = (PAST_LEN - A_WINDOW) // PAGE

    def win_logits(k, raw):
        return raw + dec_a[:, cols(win0 + k)]

    o_w = _decode_attend(qa, qa_bf, [(win_ref, k * PAGE) for k in range(A_WINDOW // PAGE)],
                         d[:, DI_KWIN:DI_KWIN + HD], d[:, DI_VWIN:DI_VWIN + HD], dec_a[:, new_col], None, win_logits)
    gate = _sigmoid(d[:, DI_MISC:DI_MISC + 3])
    o_ref[:, 0:HD] = gate[:, 0:1] * o_c + gate[:, 1:2] * o_s + gate[:, 2:3] * o_w

    qi = d[:, DI_QI:DI_QI + 64]
    wi = d[:, DI_MISC + 3:DI_MISC + 4] * 0.5
    row16 = lax.broadcasted_iota(jnp.int32, (N_PAGES, PAGE), 0)
    sc = jnp.zeros((N_PAGES, PAGE), F32)
    for k, pg in enumerate(idx_pages):
        sc_k = jnp.sum(wi * jnp.maximum(_dot_nt3(qi, pg[...]), 0.0), axis=0, keepdims=True)
        sc = jnp.where(row16 == k, sc_k, sc)
    sc_new = jnp.sum(wi * jnp.maximum(jnp.sum(qi * d[:, DI_KI:DI_KI + 64], axis=-1, keepdims=True), 0.0),
                     axis=0, keepdims=True)
    total = lambda x: jnp.sum(jnp.sum(x, axis=-1, keepdims=True), axis=0, keepdims=True)

    def count(thr, strict):
        hit = (lambda x: x > thr) if strict else (lambda x: x >= thr)
        return total(jnp.where(hit(sc), 1.0, 0.0)) + jnp.where(hit(sc_new), 1.0, 0.0)

    def max_below(bound):
        best = jnp.max(jnp.max(jnp.where(sc < bound, sc, -BIG), axis=-1, keepdims=True), axis=0, keepdims=True)
        return jnp.maximum(best, jnp.where(sc_new < bound, sc_new, -BIG))

    k_top = float(IDX_TOPK)
    lo = jnp.minimum(jnp.min(jnp.min(sc, axis=-1, keepdims=True), axis=0, keepdims=True), sc_new)
    hib = jnp.maximum(jnp.max(jnp.max(sc, axis=-1, keepdims=True), axis=0, keepdims=True), sc_new)
    hix = jnp.full((1, 1), BIG, F32)
    frac = (lax.broadcasted_iota(jnp.int32, (DEC_SPLIT, 1, 1), 0) + 1).astype(F32) * (1.0 / DEC_SPLIT)
    for _ in range(DEC_ROUNDS):
        tk = lo + (hib - lo) * frac
        cnt = jnp.sum(jnp.sum(jnp.where(sc[None] >= tk, 1.0, 0.0), axis=2, keepdims=True), axis=1, keepdims=True)
        ge = cnt + jnp.where(sc_new >= tk, 1.0, 0.0) >= k_top
        fail = jnp.min(jnp.where(ge, BIG, tk), axis=0)
        lo = jnp.max(jnp.where(ge, tk, lo), axis=0)
        hib = jnp.minimum(hib, fail)
        hix = jnp.minimum(hix, fail)

    def step_body(c):
        bound, thr, _ = c
        cand = max_below(bound)
        ok = count(cand, False) >= k_top
        return jnp.where(ok, bound, cand), jnp.where(ok, cand, thr), jnp.sum(jnp.where(ok, 0.0, 1.0))

    _, thr, _ = lax.while_loop(lambda c: c[2] > 0.5, step_body, (hix, lo, jnp.float32(1.0)))
    n_gt = count(thr, True)
    need = k_top - n_gt
    eq = jnp.where(sc == thr, 1.0, 0.0).astype(BF16)
    r128 = lax.broadcasted_iota(jnp.int32, (QB, QB), 0)
    c128 = lax.broadcasted_iota(jnp.int32, (QB, QB), 1)
    upper = jnp.where(r128 <= c128, 1.0, 0.0).astype(BF16)
    r16 = lax.broadcasted_iota(jnp.int32, (N_PAGES, N_PAGES), 0)
    c16 = lax.broadcasted_iota(jnp.int32, (N_PAGES, N_PAGES), 1)
    lower = jnp.where(c16 < r16, 1.0, 0.0).astype(BF16)
    prefix = _dot(eq, upper) + jnp.sum(_dot(lower, eq), axis=-1, keepdims=True)
    keep = jnp.where((sc > thr) | ((sc == thr) & (prefix <= need)), 1.0, 0.0)
    n_eq_past = total(eq.astype(F32))
    new_ok = (sc_new > thr) | ((sc_new == thr) & (n_gt + n_eq_past + 1.0 <= k_top))
    pos = lax.broadcasted_iota(jnp.int32, (PAGE, 2 * PAGE), 0)
    col2 = lax.broadcasted_iota(jnp.int32, (PAGE, 2 * PAGE), 1)
    keep2 = _dot(keep.astype(BF16), jnp.where(col2 // 2 == pos, 1.0, 0.0).astype(BF16))

    qb = d[:, DI_QB:DI_QB + HD] * Q_SCALE

    def b_logits(k, raw):
        return jnp.where(keep2[k:k + 1, :] > 0.5, raw + dec_b[:, cols(k)], NEG_INF)

    o_ref[:, HD:2 * HD] = _decode_attend(
        qb, qb.astype(BF16), [(pg, 0) for pg in kvb_pages], d[:, DI_KB:DI_KB + HD], d[:, DI_VB:DI_VB + HD],
        dec_b[:, new_col], new_ok, b_logits)


def _seq_page_map(k, s, pt_ref):
    return (pt_ref[s, k], 0, 0)


def _even_sample(page_table, pool_sel, pool_b, pool_idx, din, cmp_sum, win_buf, expand, dec, decc):
    specs = []
    for shape in ((4 * PAGE, HD), (4 * PAGE, HD), (PAGE, 64)):
        specs += [pl.BlockSpec((None,) + shape, functools.partial(_seq_page_map, k)) for k in range(N_PAGES)]
    specs += [
        pl.BlockSpec((None, 8, DEC_IN_W), lambda s, pt: (s, 0, 0)),
        pl.BlockSpec((None, N_CMP, 512), lambda s, pt: (s, 0, 0)),
        pl.BlockSpec((None, 4 * A_WINDOW, HD), lambda s, pt: (s, 0, 0)),
        pl.BlockSpec((N_CMP, 2 * PAST_LEN), lambda s, pt: (0, 0)),
        pl.BlockSpec((16, DEC_W), lambda s, pt: (0, 0)),
        pl.BlockSpec((8, N_CMP), lambda s, pt: (0, 0)),
    ]
    grid_spec = pltpu.PrefetchScalarGridSpec(
        num_scalar_prefetch=1, grid=(DEC_BATCH,), in_specs=specs,
        out_specs=pl.BlockSpec((None, 8, 2 * HD), lambda s, pt: (s, 0, 0)))
    return pl.pallas_call(
        _even_sample_kernel,
        grid_spec=grid_spec,
        out_shape=jax.ShapeDtypeStruct((DEC_BATCH, 8, 2 * HD), F32),
        compiler_params=_cparams(("parallel",)),
        name="even_sample",
    )(page_table, *([pool_sel] * N_PAGES), *([pool_b] * N_PAGES), *([pool_idx] * N_PAGES),
      din, cmp_sum, win_buf, expand, dec, decc)


def _odd_sample_kernel(c0_ref, c1_ref, c2_ref, qkv_ref, rrow_ref, kcol_ref, st_ref, bdil_ref, tbl_ref,
                       rope_ref, ropec_ref, gn_ref, oc_ref, od_ref, st_out_ref):
    qkv = qkv_ref[...]
    q = qkv[:, 0:HD] * Q_SCALE
    lane = lax.broadcasted_iota(jnp.int32, (QB, 8), 1)
    lane1 = lax.broadcasted_iota(jnp.int32, (1, 8), 1)
    cfg_refs = (c0_ref, c1_ref, c2_ref)
    logits = []
    for cfg in range(3):
        s = bdil_ref[cfg]
        for h in range(8):
            s_h = jnp.sum(cfg_refs[cfg][:, 2 * h, :] * q[h:h + 1, :], axis=-1, keepdims=True)
            s = s + jnp.where(lane == h, s_h, 0.0)
        logits.append(s)
    s_new = tbl_ref[0:1, 0:8]
    for h in range(8):
        s_h = jnp.sum(q[h:h + 1, :] * qkv[h:h + 1, HD:2 * HD], axis=-1, keepdims=True)
        s_new = s_new + jnp.where(lane1 == h, s_h, 0.0)
    m = s_new
    for s in logits:
        m = jnp.maximum(m, jnp.max(s, axis=0, keepdims=True))
    p_new = 3.0 * jnp.exp(s_new - m)
    probs = [jnp.exp(s - m) for s in logits]
    l = p_new
    for p in probs:
        l = l + jnp.sum(p, axis=0, keepdims=True)
    inv = 1.0 / jnp.maximum(l, 1e-30)
    for h in range(8):
        acc = p_new[:, h:h + 1] * qkv[h:h + 1, 2 * HD:3 * HD]
        for cfg in range(3):
            v = cfg_refs[cfg][:, 2 * h + 1, :]
            acc = acc + jnp.sum(probs[cfg][:, h:h + 1] * v, axis=0, keepdims=True)
        oc_ref[:, h * HD:(h + 1) * HD] = acc * inv[:, h:h + 1]

    rrow = rrow_ref[...]
    cos2, sin2 = rope_ref[0:1, :], rope_ref[1:2, :]
    qr = _rope_rows(rrow[:, 0:R_DK], cos2, sin2)
    kr = _rope_rows(rrow[:, R_DK:2 * R_DK], cos2, sin2) * RK_SCALE
    qk = jnp.sum(qr * kr, axis=-1, keepdims=True)
    kc = kcol_ref[...]
    kc = (kc * ropec_ref[:, 0:1] + jnp.concatenate([kc[64:], kc[:64]], axis=0) * ropec_ref[:, 1:2]) * RK_SCALE
    qr_bf = jnp.concatenate([qr, jnp.zeros_like(qr)], axis=0).astype(BF16)
    row8 = lax.broadcasted_iota(jnp.int32, (8, R_DV), 0)
    for h in range(R_HEADS):
        gamma = math.exp(LOG_G[h])
        state = st_ref[h]
        v = rrow[h:h + 1, 2 * R_DK:2 * R_DK + R_DV]
        q_state = jnp.sum(jnp.where(row8 == h, _dot(qr_bf, state.astype(BF16)), 0.0), axis=0, keepdims=True)
        out = qk[h:h + 1, :] * v + q_state * gamma
        st_out_ref[h] = state * gamma + kc[:, h:h + 1] * v
        od_ref[:, h * R_DV:(h + 1) * R_DV] = _readout(out, rrow[h:h + 1, 2 * R_DK + R_DV:2 * R_DK + 2 * R_DV],
                                                      gn_ref[:, h * R_DV:(h + 1) * R_DV])


def _odd_sample(c_buf, qkv, rrow, kcol, state, bdil, rel_bias, rope_row, rope_col, gn):
    views = [c_buf.reshape(DEC_BATCH, PAST_LEN // dil, dil * 16, HD) for _, dil in DIL_CFG]
    c_specs = [pl.BlockSpec((None, QB, 16, HD), functools.partial(lambda nb, s: (s, nb, 0, 0), PAST_LEN // dil // QB - 1))
               for _, dil in DIL_CFG]
    return pl.pallas_call(
        _odd_sample_kernel,
        grid=(DEC_BATCH,),
        in_specs=c_specs + [
            pl.BlockSpec((None, 8, 3 * HD), lambda s: (s, 0, 0)),
            pl.BlockSpec((None, R_HEADS, 2 * R_DK + 2 * R_DV), lambda s: (s, 0, 0)),
            pl.BlockSpec((None, R_DK, R_HEADS), lambda s: (s, 0, 0)),
            pl.BlockSpec((None, R_HEADS, R_DK, R_DV), lambda s: (s, 0, 0, 0)),
            pl.BlockSpec((3, QB, 8), lambda s: (0, 0, 0)),
            pl.BlockSpec((N_BUCKETS, 16), lambda s: (0, 0)),
            pl.BlockSpec((2, R_DK), lambda s: (0, 0)),
            pl.BlockSpec((R_DK, 2), lambda s: (0, 0)),
            pl.BlockSpec((1, 1024), lambda s: (0, 0)),
        ],
        out_specs=[pl.BlockSpec((None, 1, 1024), lambda s: (s, 0, 0)),
                   pl.BlockSpec((None, 1, 1024), lambda s: (s, 0, 0)),
                   pl.BlockSpec((None, R_HEADS, R_DK, R_DV), lambda s: (s, 0, 0, 0))],
        out_shape=[jax.ShapeDtypeStruct((DEC_BATCH, 1, 1024), F32),
                   jax.ShapeDtypeStruct((DEC_BATCH, 1, 1024), F32),
                   jax.ShapeDtypeStruct((DEC_BATCH, R_HEADS, R_DK, R_DV), F32)],
        compiler_params=_cparams(("parallel",)),
        name="odd_sample",
    )(*views, qkv, rrow, kcol, state, bdil, rel_bias, rope_row, rope_col, gn.reshape(1, 1024))


def _rope_tables(pos):
    half = R_DK // 2
    inv = 10000.0 ** (-jnp.arange(half, dtype=F32) / half)
    ang = pos.astype(F32)[:, None] * inv[None, :]
    cos, sin = jnp.cos(ang), jnp.sin(ang)
    return jnp.concatenate([cos, cos], axis=-1), jnp.concatenate([-sin, sin], axis=-1)


def _permute_even_w(w):
    qa, kva, ga, qb, kvb, qi, ki, wi = jnp.split(w, [1024, 2560, 2584, 3608, 4120, 4376, 4440], axis=1)
    pad = jnp.zeros((w.shape[0], EVEN_W - w.shape[1]), w.dtype)
    return jnp.concatenate([qa, qb, kva, kvb, qi, ki, wi, ga, pad], axis=1)


def _even_decode(zs, e, bias, page_table, cache_a_cmp, cache_a_sel, cache_b_kv, cache_b_idx, state_a_win,
                 cmp_pe, cmp_w_bf):
    dec, decc = bias[2], bias[3]
    n_pool = cache_a_cmp.shape[1]
    cmp_s = _even_first(_compress_sample(cache_a_cmp[e].reshape(n_pool, 4 * PAGE, HD), page_table, cmp_pe, cmp_w_bf),
                        DEC_BATCH)
    heads = lambda a: a.reshape(DEC_BATCH, 8, HD)
    rep = lambda a: jnp.repeat(a.reshape(DEC_BATCH, 2, HD), 4, axis=1)
    kva = zs[:, E_CMP:E_KVB].reshape(DEC_BATCH, 3, 2, 2, HD)
    kvb = zs[:, E_KVB:E_QI].reshape(DEC_BATCH, 2, 2, HD)
    sm = zs[:, E_SM:E_SM + 128]
    zero4 = lambda a: jnp.concatenate([a, jnp.zeros_like(a)], axis=1)
    qi8 = zero4(zs[:, E_QI:E_SM].reshape(DEC_BATCH, 4, 64))
    ki8 = jnp.broadcast_to(sm[:, None, SM_KI:SM_KI + 64], (DEC_BATCH, 8, 64))
    gates = sm[:, SM_GA:SM_GA + 24].reshape(DEC_BATCH, 8, 3)
    wi8 = zero4(sm[:, SM_WI:SM_WI + 4].reshape(DEC_BATCH, 4, 1))
    misc = jnp.concatenate([gates, wi8, jnp.zeros((DEC_BATCH, 8, DEC_IN_W - DI_MISC - 4), F32)], axis=2)
    din = jnp.concatenate([
        heads(zs[:, E_QA:E_QB]), heads(zs[:, E_QB:E_CMP]),
        rep(kva[:, 1, :, 0]), rep(kva[:, 1, :, 1]), rep(kva[:, 2, :, 0]), rep(kva[:, 2, :, 1]),
        rep(kvb[:, :, 0]), rep(kvb[:, :, 1]), qi8, ki8, misc], axis=2)
    expand = (jnp.arange(2 * PAST_LEN)[None, :] // (2 * SEL_BLOCK) == jnp.arange(N_CMP)[:, None]).astype(BF16)
    o_s = _even_sample(page_table, cache_a_sel[e].reshape(n_pool, 4 * PAGE, HD),
                       cache_b_kv[e].reshape(n_pool, 4 * PAGE, HD), cache_b_idx[e],
                       din, cmp_s, state_a_win[e].reshape(DEC_BATCH, 4 * A_WINDOW, HD), expand, dec, decc)
    return o_s[:, :, :HD].reshape(DEC_BATCH, 1024), o_s[:, :, HD:].reshape(DEC_BATCH, 1024)


def _even_layer(x, e, bias, page_table, cache_a_cmp, cache_a_sel, cache_b_kv, cache_b_idx, state_a_win,
                g, w_in, w_out, cmp_pe, cmp_w):
    bnear, bcmp = bias[0], bias[1]
    z, zb = _proj_in(x, g, _permute_even_w(w_in).astype(BF16), 896)
    cmp_w_bf = cmp_w.astype(BF16)
    cmp_p = _even_first(_compress_prompt(z, cmp_pe, cmp_w_bf), BATCH)
    oa_p = _nsa_prompt(z, zb, cmp_p, bnear[:, 0:8], bcmp)
    ob_p = _dsa_prompt(z, zb, bnear[:, 8:16])
    zs = z[N_TOK_P:]
    oa_s, ob_s = _even_decode(zs, e, bias, page_table, cache_a_cmp, cache_a_sel, cache_b_kv, cache_b_idx,
                              state_a_win, cmp_pe, cmp_w_bf)
    oa = jnp.concatenate([oa_p, oa_s], axis=0)
    ob = jnp.concatenate([ob_p, ob_s], axis=0)
    x = _proj_out(x, oa, ob, w_out[:1024].astype(BF16), w_out[1024:].astype(BF16))

    kv5 = lambda a, n: a.reshape(n, -1, 2, 2, HD)
    zp = lambda lo, hi: z[:N_TOK_P, lo:hi]
    state = (
        kv5(zp(E_CMP, E_SEL), BATCH), kv5(zs[:, E_CMP:E_SEL], DEC_BATCH),
        kv5(zp(E_SEL, E_WIN), BATCH), kv5(zs[:, E_SEL:E_WIN], DEC_BATCH),
        kv5(zp(E_KVB, E_QI), BATCH), kv5(zs[:, E_KVB:E_QI], DEC_BATCH),
        zp(E_SM, E_SM + 64).reshape(BATCH, SEQ, 64), zs[:, None, E_SM:E_SM + 64],
        kv5(zp(E_WIN, E_KVB).reshape(BATCH, SEQ, 512)[:, SEQ - A_WINDOW:], BATCH), kv5(zs[:, E_WIN:E_KVB], DEC_BATCH),
    )
    return x, state


def _odd_layer(x, o, bias, state_c_win, state_ret, g, w_in, w_out, gn):
    bnear, bdil = bias[0], bias[4]
    z, zb = _proj_in(x, g, w_in.astype(BF16), 1536)
    cos_p, sin_p = _rope_tables(jnp.arange(SEQ))
    oc_p = _dil_prompt(z, zb, bnear[:, 0:8])
    od_p, ret_p = _ret_prompt(z, cos_p, sin_p, gn)

    zs = z[N_TOK_P:]
    kvc = zs[:, O_KVC:O_QR].reshape(DEC_BATCH, 8, 2, HD)
    qkv = jnp.concatenate([zs[:, O_QC:O_KVC].reshape(DEC_BATCH, 8, HD), kvc[:, :, 0], kvc[:, :, 1]], axis=2)
    rrow = jnp.concatenate([zs[:, O_QR:O_KR].reshape(DEC_BATCH, R_HEADS, R_DK),
                            zs[:, O_KR:O_VR].reshape(DEC_BATCH, R_HEADS, R_DK),
                            zs[:, O_VR:O_GR].reshape(DEC_BATCH, R_HEADS, R_DV),
                            zs[:, O_GR:].reshape(DEC_BATCH, R_HEADS, R_DV)], axis=2)
    kcol = jnp.swapaxes(zs[:, O_KR:O_VR].reshape(DEC_BATCH, R_HEADS, R_DK), 1, 2)
    cos_s, sin_s = _rope_tables(jnp.full((1,), PAST_LEN))
    rope_row = jnp.concatenate([cos_s, sin_s], axis=0)
    oc_s, od_s, ret_s = _odd_sample(state_c_win[o], qkv, rrow, kcol,
                                    state_ret[o], bdil, bias[5], rope_row, rope_row.T, gn)
    oc = jnp.concatenate([oc_p, oc_s.reshape(DEC_BATCH, 1024)], axis=0)
    od = jnp.concatenate([od_p, od_s.reshape(DEC_BATCH, 1024)], axis=0)
    x = _proj_out(x, oc, od, w_out[:1024].astype(BF16), w_out[1024:].astype(BF16))
    state = (z[:N_TOK_P, O_KVC:O_QR].reshape(BATCH, SEQ, 8, 2, HD), kvc[:, None], ret_p, ret_s)
    return x, state


def kernel(x_prompt, x_sample, cache_a_cmp, cache_a_sel, cache_b_kv, cache_b_idx, state_a_win, state_c_win,
           state_ret, page_table, p_prompt, p_sample, rel_bias, norm_g, final_norm, ffn_w_in, ffn_w_out,
           ple_gate, ple_proj, even_w_in, even_w_out, nsa_cmp_pe, nsa_cmp_w, odd_w_in, odd_w_out, ret_gn):
    depth = norm_g.shape[0]
    x = jnp.concatenate([x_prompt.reshape(N_TOK_P, D_MODEL), x_sample.reshape(DEC_BATCH, D_MODEL)], axis=0)
    bias = tuple(_bias_tables(rel_bias)) + (rel_bias,)
    w_in_bf = ffn_w_in.astype(BF16)
    w_out_bf = ffn_w_out.astype(BF16)
    even_states, odd_states = [], []
    for i in range(depth):
        x = _ffn_half(x, norm_g[i, 0], w_in_bf, w_out_bf, i, 0)
        if i % 2 == 0:
            e = i // 2
            x, st = _even_layer(x, e, bias, page_table, cache_a_cmp, cache_a_sel, cache_b_kv, cache_b_idx,
                                state_a_win, norm_g[i, 1], even_w_in[e], even_w_out[e], nsa_cmp_pe[e], nsa_cmp_w[e])
            even_states.append(st)
        else:
            o = i // 2
            x, st = _odd_layer(x, o, bias, state_c_win, state_ret, norm_g[i, 1], odd_w_in[o], odd_w_out[o],
                               ret_gn[o])
            odd_states.append(st)
        x = _ffn_half(x, norm_g[i, 2], w_in_bf, w_out_bf, i, 1)
        p = jnp.concatenate([p_prompt[i].reshape(N_TOK_P, PLE_DIM), p_sample[i].reshape(DEC_BATCH, PLE_DIM)], axis=0)
        x = _ple(x, p, norm_g[i, 3], ple_gate[i].astype(BF16), ple_proj[i].astype(BF16), final_norm,
                 final=(i == depth - 1))
    y_prompt = x[:N_TOK_P].reshape(BATCH, SEQ, D_MODEL)
    y_sample = x[N_TOK_P:].reshape(DEC_BATCH, 1, D_MODEL)
    ev = [jnp.stack([st[k] for st in even_states]) for k in range(10)]
    od = [jnp.stack([st[k] for st in odd_states]) for k in range(4)]
    return (y_prompt, y_sample, *ev, *od)
```

```python
import functools
import math

import numpy as np
import jax
import jax.numpy as jnp
from jax import lax
from jax.experimental import pallas as pl
from jax.experimental.pallas import tpu as pltpu

F32 = jnp.float32
BF16 = jnp.bfloat16
HI = lax.Precision.HIGHEST

D_MODEL = 2048
BATCH = 4
SEQ = 2048
DEC_BATCH = 128
PAST_LEN = 2048
PAGE = 128
N_PAGES = PAST_LEN // PAGE
HD = 128
D_FF = 5632
PLE_DIM = 256
N_TOK_P = BATCH * SEQ
N_TOK = N_TOK_P + DEC_BATCH
QB = 128
NQB = SEQ // QB
CMP_BLOCK = 32
N_CMP = SEQ // CMP_BLOCK
SEL_BLOCK = 64
N_SEL = 16
A_WINDOW = 512
IDX_TOPK = 256
R_HEADS = 4
R_DK = 128
R_DV = 256
N_BUCKETS = 32
RMS_EPS = 1e-6
NEG_INF = -1e30
BIG = 3.0e38
FORCE_SCORE = 1e4
Q_SCALE = HD ** -0.5
LOG2E = math.log2(math.e)
RK_SCALE = R_DK ** -0.5
EVEN_W = 4480
ODD_W = 6144
VMEM_LIMIT = 56 * 1024 * 1024
BISECT_STEPS = 16

E_QA, E_QB, E_CMP, E_SEL, E_WIN, E_KVB, E_QI, E_SM = 0, 1024, 2048, 2560, 3072, 3584, 4096, 4352
SM_KI, SM_WI, SM_GA = 0, 64, 68
O_QC, O_KVC, O_QR, O_KR, O_VR, O_GR = 0, 1024, 3072, 3584, 4096, 5120

LOG_G = [math.log1p(-(2.0 ** (-5.0 - h))) for h in range(R_HEADS)]


def _cparams(sem=None):
    return pltpu.CompilerParams(dimension_semantics=sem, vmem_limit_bytes=VMEM_LIMIT)


def _dot(a, b, precision=None):
    return jnp.dot(a, b, preferred_element_type=F32, precision=precision)


def _dot_nt(a, b, precision=None):
    return lax.dot_general(a, b, (((1,), (1,)), ((), ())), preferred_element_type=F32, precision=precision)


def _sigmoid(x):
    return 1.0 / (1.0 + jnp.exp(-x))


def _rms(x, g):
    return x * lax.rsqrt(jnp.mean(x * x, axis=-1, keepdims=True) + RMS_EPS) * g


FFN_TM, FFN_TF = 640, 512


def _ffn_kernel(x_ref, g_ref, wg_ref, wu_ref, wo_ref, o_ref, h_scr, acc_scr):
    j = pl.program_id(1)

    @pl.when(j == 0)
    def _():
        h_scr[...] = _rms(x_ref[...], g_ref[...]).astype(BF16)
        acc_scr[...] = jnp.zeros_like(acc_scr)

    h = h_scr[...]
    gate = _dot(h, wg_ref[...])
    up = _dot(h, wu_ref[...])
    act = gate * _sigmoid(gate) * up
    acc_scr[...] += _dot(act.astype(BF16), wo_ref[...])

    @pl.when(j == pl.num_programs(1) - 1)
    def _():
        o_ref[...] = x_ref[...] + 0.5 * acc_scr[...]


def _ffn_half(x, g, w_in, w_out, layer, half):
    nj = D_FF // FFN_TF
    return pl.pallas_call(
        _ffn_kernel,
        grid=(N_TOK // FFN_TM, nj),
        in_specs=[
            pl.BlockSpec((FFN_TM, D_MODEL), lambda i, j: (i, 0)),
            pl.BlockSpec((1, D_MODEL), lambda i, j: (0, 0)),
            pl.BlockSpec((None, None, D_MODEL, FFN_TF), lambda i, j: (layer, half, 0, j)),
            pl.BlockSpec((None, None, D_MODEL, FFN_TF), lambda i, j: (layer, half, 0, j + nj)),
            pl.BlockSpec((None, None, FFN_TF, D_MODEL), lambda i, j: (layer, half, j, 0)),
        ],
        out_specs=pl.BlockSpec((FFN_TM, D_MODEL), lambda i, j: (i, 0)),
        out_shape=jax.ShapeDtypeStruct((N_TOK, D_MODEL), F32),
        scratch_shapes=[pltpu.VMEM((FFN_TM, D_MODEL), BF16), pltpu.VMEM((FFN_TM, D_MODEL), F32)],
        compiler_params=_cparams(("parallel", "arbitrary")),
        name="ffn_half",
    )(x, g.reshape(1, D_MODEL), w_in, w_in, w_out)


PROJ_TM = 640


def _proj_in_kernel(x_ref, g_ref, w_ref, z_ref, zb_ref, h_scr):
    @pl.when(pl.program_id(1) == 0)
    def _():
        h_scr[...] = _rms(x_ref[...], g_ref[...]).astype(BF16)

    z = _dot(h_scr[...], w_ref[...])
    z_ref[...] = z
    zb_ref[...] = z.astype(BF16)


def _proj_in(x, g, w, tn):
    n = w.shape[1]
    return pl.pallas_call(
        _proj_in_kernel,
        grid=(N_TOK // PROJ_TM, n // tn),
        in_specs=[
            pl.BlockSpec((PROJ_TM, D_MODEL), lambda i, j: (i, 0)),
            pl.BlockSpec((1, D_MODEL), lambda i, j: (0, 0)),
            pl.BlockSpec((D_MODEL, tn), lambda i, j: (0, j)),
        ],
        out_specs=[pl.BlockSpec((PROJ_TM, tn), lambda i, j: (i, j)),
                   pl.BlockSpec((PROJ_TM, tn), lambda i, j: (i, j))],
        out_shape=[jax.ShapeDtypeStruct((N_TOK, n), F32), jax.ShapeDtypeStruct((N_TOK, n), BF16)],
        scratch_shapes=[pltpu.VMEM((PROJ_TM, D_MODEL), BF16)],
        compiler_params=_cparams(("parallel", "arbitrary")),
        name="proj_in",
    )(x, g.reshape(1, D_MODEL), w)


OUT_TM = 320


def _proj_out_kernel(x_ref, a_ref, b_ref, wa_ref, wb_ref, o_ref):
    o_ref[...] = (x_ref[...] + _dot(a_ref[...].astype(BF16), wa_ref[...])
                  + _dot(b_ref[...].astype(BF16), wb_ref[...]))


def _proj_out(x, a, b, wa, wb):
    ka, kb = a.shape[1], b.shape[1]
    return pl.pallas_call(
        _proj_out_kernel,
        grid=(N_TOK // OUT_TM,),
        in_specs=[
            pl.BlockSpec((OUT_TM, D_MODEL), lambda i: (i, 0)),
            pl.BlockSpec((OUT_TM, ka), lambda i: (i, 0)),
            pl.BlockSpec((OUT_TM, kb), lambda i: (i, 0)),
            pl.BlockSpec((ka, D_MODEL), lambda i: (0, 0)),
            pl.BlockSpec((kb, D_MODEL), lambda i: (0, 0)),
        ],
        out_specs=pl.BlockSpec((OUT_TM, D_MODEL), lambda i: (i, 0)),
        out_shape=jax.ShapeDtypeStruct((N_TOK, D_MODEL), F32),
        compiler_params=_cparams(("parallel",)),
        name="proj_out",
    )(x, a, b, wa, wb)


def _ple_kernel(x_ref, p_ref, g_ref, wg_ref, wp_ref, gf_ref, o_ref, *, final):
    x = x_ref[...]
    gate = _sigmoid(_dot(_rms(x, g_ref[...]).astype(BF16), wg_ref[...]))
    y = x + gate * _dot(p_ref[...].astype(BF16), wp_ref[...])
    o_ref[...] = _rms(y, gf_ref[...]) if final else y


def _ple(x, p, g, w_gate, w_proj, g_final, final):
    return pl.pallas_call(
        functools.partial(_ple_kernel, final=final),
        grid=(N_TOK // OUT_TM,),
        in_specs=[
            pl.BlockSpec((OUT_TM, D_MODEL), lambda i: (i, 0)),
            pl.BlockSpec((OUT_TM, PLE_DIM), lambda i: (i, 0)),
            pl.BlockSpec((1, D_MODEL), lambda i: (0, 0)),
            pl.BlockSpec((D_MODEL, D_MODEL), lambda i: (0, 0)),
            pl.BlockSpec((PLE_DIM, D_MODEL), lambda i: (0, 0)),
            pl.BlockSpec((1, D_MODEL), lambda i: (0, 0)),
        ],
        out_specs=pl.BlockSpec((OUT_TM, D_MODEL), lambda i: (i, 0)),
        out_shape=jax.ShapeDtypeStruct((N_TOK, D_MODEL), F32),
        compiler_params=_cparams(("parallel",)),
        name="ple_add",
    )(x, p, g.reshape(1, D_MODEL), w_gate, w_proj, g_final.reshape(1, D_MODEL))


def _t5_bucket_np(dist):
    dist = np.maximum(np.asarray(dist, np.int64), 0)
    exact = N_BUCKETS // 2
    out = {}
    for dt in (np.float32, np.float64):
        scaled = np.log(np.maximum(dist, exact).astype(dt) / dt(exact)) / dt(math.log(128 / exact))
        large = np.minimum(exact + (scaled * dt(N_BUCKETS - exact)).astype(np.int64), N_BUCKETS - 1)
        out[dt] = np.where(dist < exact, dist, large)
    assert np.array_equal(out[np.float32], out[np.float64])
    return out[np.float32].astype(np.int32)


def _cmp_block_of_col(col):
    return 2 * (col % 32) + col // 32


def _bias_kernel(tbl_ref, bk_near, bk_cmp, bk_dec, bk_decc, bk_dil, o_near, o_cmp, o_dec, o_decc, o_dil):
    def lookup(bucket, h):
        acc = jnp.zeros(bucket.shape, F32)
        for b in range(N_BUCKETS):
            acc = jnp.where(bucket == b, tbl_ref[b, h], acc)
        return acc

    for rel in range(3):
        bk = bk_near[rel]
        for h in range(16):
            o_near[rel, h] = lookup(bk, h) * LOG2E
    bk = bk_cmp[...]
    for h in range(8):
        o_cmp[h] = lookup(bk, h)
    bk = bk_dec[...]
    for h in range(16):
        o_dec[h:h + 1, :] = lookup(bk, h)
    bk = bk_decc[...]
    for h in range(8):
        o_decc[h:h + 1, :] = lookup(bk, h)
    lane = lax.broadcasted_iota(jnp.int32, (QB, 8), 1)
    for cfg in range(3):
        bk = bk_dil[cfg]
        acc = jnp.zeros((QB, 8), F32)
        for h in range(8):
            acc = jnp.where(lane == h, lookup(bk, h), acc)
        o_dil[cfg] = acc


DEC_W = 2 * 2176


def _bias_tables(rel_bias):
    r = np.arange(QB)[:, None]
    c = np.arange(QB)[None, :]
    bk_near = np.stack([_t5_bucket_np(rel * QB + r - c) for rel in range(3)])
    t = np.arange(SEQ)[:, None]
    col = np.arange(N_CMP)[None, :]
    bk_cmp = _t5_bucket_np(t - (CMP_BLOCK * _cmp_block_of_col(col) + CMP_BLOCK - 1))
    bk_dec = _t5_bucket_np(PAST_LEN - np.arange(DEC_W)[None, :] // 2)
    bk_decc = _t5_bucket_np(PAST_LEN - (CMP_BLOCK * _cmp_block_of_col(col) + CMP_BLOCK - 1))
    i = np.arange(QB)[:, None]
    bk_dil = np.stack([np.broadcast_to(_t5_bucket_np(dil * (QB - i)), (QB, 8)) for dil in (1, 4, 16)])
    vm = pl.BlockSpec(memory_space=pltpu.VMEM)
    return pl.pallas_call(
        _bias_kernel,
        in_specs=[pl.BlockSpec(memory_space=pltpu.SMEM), vm, vm, vm, vm, vm],
        out_specs=[vm] * 5,
        out_shape=[
            jax.ShapeDtypeStruct((3, 16, QB, QB), F32),
            jax.ShapeDtypeStruct((8, SEQ, N_CMP), F32),
            jax.ShapeDtypeStruct((16, DEC_W), F32),
            jax.ShapeDtypeStruct((8, N_CMP), F32),
            jax.ShapeDtypeStruct((3, QB, 8), F32),
        ],
        compiler_params=_cparams(),
        name="bias_tables",
    )(rel_bias, jnp.asarray(bk_near), jnp.asarray(bk_cmp), jnp.asarray(bk_dec), jnp.asarray(bk_decc),
      jnp.asarray(bk_dil))


def _compress_rows(rows_ref, pe_ref, w_ref, out_ref, nblk):
    for c in range(2):
        acc = jnp.zeros((2 * nblk, HD), F32)
        for l in range(CMP_BLOCK):
            pe = pe_ref[c, l:l + 1, :]
            xs = [rows_ref[pl.ds(4 * l + 2 * g + c, nblk, stride=4 * CMP_BLOCK), :] + pe for g in range(2)]
            acc = acc + _dot(jnp.concatenate(xs, axis=0).astype(BF16), w_ref[c, l])
        for g in range(2):
            out_ref[:, (2 * g + c) * HD:(2 * g + c + 1) * HD] = acc[g * nblk:(g + 1) * nblk]


CMP_ROWS_P = 4096


def _compress_prompt_kernel(rows_ref, pe_ref, w_ref, out_ref):
    _compress_rows(rows_ref, pe_ref, w_ref, out_ref, CMP_ROWS_P // CMP_BLOCK)


def _compress_prompt(z, pe, w):
    nblk = CMP_ROWS_P // CMP_BLOCK
    rows = z[:N_TOK_P, E_CMP:E_SEL].reshape(4 * N_TOK_P, HD)
    return pl.pallas_call(
        _compress_prompt_kernel,
        grid=(N_TOK_P // CMP_ROWS_P,),
        in_specs=[
            pl.BlockSpec((4 * CMP_ROWS_P, HD), lambda i: (i, 0)),
            pl.BlockSpec((2, CMP_BLOCK, HD), lambda i: (0, 0, 0)),
            pl.BlockSpec((2, CMP_BLOCK, HD, HD), lambda i: (0, 0, 0, 0)),
        ],
        out_specs=pl.BlockSpec((nblk, 512), lambda i: (i, 0)),
        out_shape=jax.ShapeDtypeStruct((N_TOK_P // CMP_BLOCK, 512), F32),
        compiler_params=_cparams(("parallel",)),
        name="nsa_compress_prompt",
    )(rows, pe, w)


CMP_SEQS = 4
CMP_COPIES = CMP_SEQS * N_PAGES


def _compress_sample_kernel(pt_ref, pool_ref, pe_ref, w_ref, out_ref, rows_scr, sem):
    i = pl.program_id(0)

    def page_copy(step, slot, j):
        page = pt_ref[step * CMP_SEQS + j // N_PAGES, j % N_PAGES]
        return pltpu.make_async_copy(pool_ref.at[page], rows_scr.at[slot, pl.ds(j * 4 * PAGE, 4 * PAGE)],
                                     sem.at[slot])

    def start_all(step, slot):
        for j in range(CMP_COPIES):
            page_copy(step, slot, j).start()

    @pl.when(i == 0)
    def _():
        start_all(0, 0)

    @pl.when(i + 1 < pl.num_programs(0))
    def _():
        start_all(i + 1, (i + 1) % 2)

    slot = i % 2
    for j in range(CMP_COPIES):
        page_copy(i, slot, j).wait()
    _compress_rows(rows_scr.at[slot], pe_ref, w_ref, out_ref, CMP_SEQS * N_CMP)


def _compress_sample(pool, page_table, pe, w):
    nblk = CMP_SEQS * N_CMP
    grid_spec = pltpu.PrefetchScalarGridSpec(
        num_scalar_prefetch=1,
        grid=(DEC_BATCH // CMP_SEQS,),
        in_specs=[
            pl.BlockSpec(memory_space=pl.ANY),
            pl.BlockSpec((2, CMP_BLOCK, HD), lambda i, pt: (0, 0, 0)),
            pl.BlockSpec((2, CMP_BLOCK, HD, HD), lambda i, pt: (0, 0, 0, 0)),
        ],
        out_specs=pl.BlockSpec((nblk, 512), lambda i, pt: (i, 0)),
        scratch_shapes=[pltpu.VMEM((2, 4 * CMP_SEQS * PAST_LEN, HD), F32), pltpu.SemaphoreType.DMA((2,))],
    )
    return pl.pallas_call(
        _compress_sample_kernel,
        grid_spec=grid_spec,
        out_shape=jax.ShapeDtypeStruct((DEC_BATCH * N_CMP, 512), F32),
        compiler_params=_cparams(("arbitrary",)),
        name="nsa_compress_sample",
    )(page_table, pool, pe, w)


def _even_first(cmp_sum, n_seq):
    x = cmp_sum.reshape(n_seq, N_CMP // 2, 2, 512)
    return jnp.swapaxes(x, 1, 2).reshape(n_seq, N_CMP, 512)


RG = 512
N_RG = SEQ // RG
RG_CHUNKS = RG // QB


def _attend_regions(i, streams, st, slot0=0, min_chunk=None, qk_first=False):
    m_scr, acc_scr = st
    ones = jnp.ones((RG, HD), BF16)
    for j in range(len(streams)):
        m_scr[slot0 + j] = jnp.full(m_scr.shape[1:], NEG_INF, F32)
        acc_scr[slot0 + j] = jnp.zeros(acc_scr.shape[1:], F32)
    for rg in range(N_RG):
        visit = rg * RG_CHUNKS <= i
        if min_chunk is not None:
            visit = visit & ((rg + 1) * RG_CHUNKS - 1 >= min_chunk)

        @pl.when(visit)
        def _():
            rows = slice(rg * RG, (rg + 1) * RG)
            qk = lambda q_bf, kv_ref, kcol: _dot_nt(q_bf, kv_ref[rows, kcol:kcol + HD])

            def pv(slot, p_bf, alpha, kv_ref, vcol):
                v1 = jnp.concatenate([kv_ref[rows, vcol:vcol + HD], ones], axis=1)
                acc_scr[slot] = alpha * acc_scr[slot] + _dot(p_bf, v1)

            raw = [qk(*stream[:3]) for stream in streams] if qk_first else None
            probs = []
            for j, (q_bf, kv_ref, kcol, vcol, logit_fn) in enumerate(streams):
                slot = slot0 + j
                s = logit_fn(rg, raw[j] if qk_first else qk(q_bf, kv_ref, kcol))
                m_old = m_scr[slot]
                m_new = jnp.maximum(m_old, jnp.max(s, axis=-1, keepdims=True))
                alpha = jnp.exp2(m_old - m_new)
                p_bf = jnp.exp2(s - m_new).astype(BF16)
                m_scr[slot] = m_new
                if qk_first:
                    probs.append((slot, p_bf, alpha, kv_ref, vcol))
                else:
                    pv(slot, p_bf, alpha, kv_ref, vcol)
            for args in probs:
                pv(*args)

    outs = []
    for j in range(len(streams)):
        acc = acc_scr[slot0 + j]
        outs.append(acc[:, :HD] / jnp.maximum(acc[:, HD:], 1e-30))
    return outs


def _softmax_state(slots, rows):
    return [pltpu.VMEM((slots, rows, 1), F32), pltpu.VMEM((slots, rows, 2 * HD), F32)]


def _mask_rows4(ok, logits):
    w = logits.shape[1]
    return jnp.where(ok[None], logits.reshape(4, QB, w), NEG_INF).reshape(4 * QB, w)


def _stack_heads(q_ref, g):
    return jnp.concatenate([q_ref[:, (4 * g + a) * HD:(4 * g + a + 1) * HD] for a in range(4)], axis=0)


def _rel_dist(i, kb):
    r = lax.broadcasted_iota(jnp.int32, (QB, QB), 0)
    c = lax.broadcasted_iota(jnp.int32, (QB, QB), 1)
    return (i - kb) * QB + r - c


def _region_dist(i, rg):
    r = lax.broadcasted_iota(jnp.int32, (QB, RG), 0)
    c = lax.broadcasted_iota(jnp.int32, (QB, RG), 1)
    return i * QB - rg * RG + r - c


def _region_bias(bnear_ref, i, rg, heads):
    tiles = [bnear_ref[jnp.clip(i - (rg * RG_CHUNKS + c), 0, 2), heads].reshape(-1, QB) for c in range(RG_CHUNKS)]
    return jnp.concatenate(tiles, axis=1)


def _split_bf16(a):
    hi = a.astype(BF16)
    return hi, (a - hi.astype(F32)).astype(BF16)


def _dot_nt3(a, b):
    ah, al = _split_bf16(a)
    bh, bl = _split_bf16(b)
    return _dot_nt(ah, bh) + (_dot_nt(ah, bl) + _dot_nt(al, bh))


def _rank_select(score, n_blocks, n_take):
    blk = lax.broadcasted_iota(jnp.int32, score.shape, 1)
    rank = jnp.zeros(score.shape, F32)
    for j in range(n_blocks):
        col = score[:, j:j + 1]
        beats = (col > score) | ((col == score) & (blk > j))
        rank = rank + jnp.where(beats, 1.0, 0.0)
    return jnp.where(rank < n_take, 1.0, 0.0)


def _nsa_prompt_kernel(q_ref, sm_ref, ksel_ref, kwin_ref, cmp_ref, bnear_ref, bcmp_ref, o_ref,
                       selk_scr, m_scr, acc_scr):
    i = pl.program_id(1)
    st = (m_scr, acc_scr)
    sm = sm_ref[...]
    jj = lax.broadcasted_iota(jnp.int32, (N_CMP // 2, SEQ), 0)
    cc = lax.broadcasted_iota(jnp.int32, (N_CMP // 2, SEQ), 1)
    expand = jnp.where(jj == cc // SEL_BLOCK, 1.0, 0.0).astype(BF16)
    row = lax.broadcasted_iota(jnp.int32, (4 * QB, N_CMP), 0) & (QB - 1)
    col = lax.broadcasted_iota(jnp.int32, (4 * QB, N_CMP), 1)
    dist_c = i * QB + row - (CMP_BLOCK * _cmp_block_of_col(col) + CMP_BLOCK - 1)
    ok_c = dist_c >= 0
    t_q = i * QB + lax.broadcasted_iota(jnp.int32, (QB, N_CMP // 2), 0)
    blk = lax.broadcasted_iota(jnp.int32, (QB, N_CMP // 2), 1)
    cur = t_q // SEL_BLOCK
    forced = (blk == 0) | (blk == cur) | (blk == cur - 1)
    admissible = blk * SEL_BLOCK <= t_q
    q_bf, o_c = [], []
    for g in range(2):
        q = _stack_heads(q_ref, g) * Q_SCALE
        q_bf.append((q * LOG2E).astype(BF16))
        logit = _dot_nt3(q, cmp_ref[:, 2 * g * HD:(2 * g + 1) * HD]) + bcmp_ref[4 * g:4 * g + 4].reshape(4 * QB, N_CMP)
        logit = jnp.where(ok_c, logit, NEG_INF)
        e = jnp.where(ok_c, jnp.exp(logit - jnp.max(logit, axis=-1, keepdims=True)), 0.0)
        p_c = e / jnp.maximum(jnp.sum(e, axis=-1, keepdims=True), 1e-30)
        o_c.append(_dot(p_c.astype(BF16), cmp_ref[:, (2 * g + 1) * HD:(2 * g + 2) * HD].astype(BF16)))
        p_g = p_c[0:QB] + p_c[QB:2 * QB] + p_c[2 * QB:3 * QB] + p_c[3 * QB:4 * QB]
        score = p_g[:, :N_CMP // 2] + p_g[:, N_CMP // 2:]
        score = jnp.where(forced, FORCE_SCORE, score)
        score = jnp.where(admissible, score, NEG_INF)
        sel = _rank_select(score, SEQ // SEL_BLOCK, N_SEL).astype(BF16)
        selk_scr[g] = _dot(sel, expand)

    def sel_logits(g, rg, raw):
        ok = (selk_scr[g, :, rg * RG:(rg + 1) * RG] > 0.5) & (_region_dist(i, rg) >= 0)
        return _mask_rows4(ok, raw + _region_bias(bnear_ref, i, rg, slice(4 * g, 4 * g + 4)))

    def win_logits(g, rg, raw):
        d = _region_dist(i, rg)
        return _mask_rows4((d >= 0) & (d <= A_WINDOW), raw + _region_bias(bnear_ref, i, rg, slice(4 * g, 4 * g + 4)))

    o_s = _attend_regions(i, [(q_bf[g], ksel_ref, 2 * g * HD, (2 * g + 1) * HD, functools.partial(sel_logits, g))
                              for g in range(2)], st, slot0=0)
    o_w = _attend_regions(i, [(q_bf[g], kwin_ref, 2 * g * HD, (2 * g + 1) * HD, functools.partial(win_logits, g))
                              for g in range(2)], st, slot0=2, min_chunk=i - A_WINDOW // QB)
    for g in range(2):
        gates = []
        for br in range(3):
            gates.append(jnp.concatenate(
                [_sigmoid(sm[:, SM_GA + 3 * (4 * g + a) + br:SM_GA + 3 * (4 * g + a) + br + 1]) for a in range(4)],
                axis=0))
        out = gates[0] * o_c[g] + gates[1] * o_s[g] + gates[2] * o_w[g]
        for a in range(4):
            o_ref[:, (4 * g + a) * HD:(4 * g + a + 1) * HD] = out[a * QB:(a + 1) * QB]


def _nsa_prompt(z, zb, cmp_sum, bnear, bcmp):
    return pl.pallas_call(
        _nsa_prompt_kernel,
        grid=(BATCH, NQB),
        in_specs=[
            pl.BlockSpec((QB, 1024), lambda b, i: (b * NQB + i, E_QA // 1024)),
            pl.BlockSpec((QB, 128), lambda b, i: (b * NQB + i, E_SM // 128)),
            pl.BlockSpec((SEQ, 512), lambda b, i: (b, E_SEL // 512)),
            pl.BlockSpec((SEQ, 512), lambda b, i: (b, E_WIN // 512)),
            pl.BlockSpec((None, N_CMP, 512), lambda b, i: (b, 0, 0)),
            pl.BlockSpec((3, 8, QB, QB), lambda b, i: (0, 0, 0, 0)),
            pl.BlockSpec((8, QB, N_CMP), lambda b, i: (0, i, 0)),
        ],
        out_specs=pl.BlockSpec((QB, 1024), lambda b, i: (b * NQB + i, 0)),
        out_shape=jax.ShapeDtypeStruct((N_TOK_P, 1024), F32),
        scratch_shapes=[pltpu.VMEM((2, QB, SEQ), F32)] + _softmax_state(4, 4 * QB),
        compiler_params=_cparams(("parallel", "arbitrary")),
        name="nsa_prompt",
    )(z, z, zb, zb, cmp_sum, bnear, bcmp)


def _dsa_prompt_kernel(q_ref, qi_ref, sm_ref, kidx_ref, kv_ref, bnear_ref, o_ref,
                       sc_scr, mk_scr, m_scr, acc_scr):
    i = pl.program_id(1)
    sm = sm_ref[...]
    qi = qi_ref[...]
    wi = sm[:, SM_WI:SM_WI + 4] * 0.5

    for rg in range(N_RG):
        cols = slice(rg * RG, (rg + 1) * RG)

        @pl.when(rg * RG_CHUNKS <= i)
        def _():
            kid = kidx_ref[cols, SM_KI:SM_KI + 64]
            sc = jnp.zeros((QB, RG), F32)
            for h in range(4):
                sc = sc + wi[:, h:h + 1] * jnp.maximum(_dot_nt3(qi[:, 64 * h:64 * h + 64], kid), 0.0)
            sc_scr[:, cols] = jnp.where(_region_dist(i, rg) >= 0, sc, NEG_INF)

        @pl.when(rg * RG_CHUNKS > i)
        def _():
            sc_scr[:, cols] = jnp.full((QB, RG), NEG_INF, F32)

    scores = sc_scr[...]
    lo = jnp.min(jnp.where(scores > 0.5 * NEG_INF, scores, BIG), axis=-1, keepdims=True)
    hib = jnp.max(scores, axis=-1, keepdims=True)
    n_causal = (i * QB + 1 + lax.broadcasted_iota(jnp.int32, (QB, 1), 0)).astype(F32)
    k_eff = jnp.minimum(n_causal, float(IDX_TOPK))

    def count(thr, strict):
        s = sc_scr[...]
        return jnp.sum(jnp.where((s > thr) if strict else (s >= thr), 1.0, 0.0), axis=-1, keepdims=True)

    def max_below(bound):
        s = sc_scr[...]
        return jnp.max(jnp.where(s < bound, s, -BIG), axis=-1, keepdims=True)

    hix = jnp.full((QB, 1), BIG, F32)
    for _ in range(BISECT_STEPS):
        mid = 0.5 * (lo + hib)
        ge = count(mid, False) >= k_eff
        lo = jnp.where(ge, mid, lo)
        hib = jnp.where(ge, hib, mid)
        hix = jnp.where(ge, hix, mid)

    def step_cond(c):
        return c[3] > 0.5

    def step_body(c):
        bound, thr, done, _ = c
        cand = max_below(bound)
        ok = count(cand, False) >= k_eff
        active = done < 0.5
        thr = jnp.where(active & ok, cand, thr)
        bound = jnp.where(active & jnp.logical_not(ok), cand, bound)
        done = jnp.where(ok, 1.0, done)
        return bound, thr, done, jnp.sum(1.0 - done)

    _, thr, _, _ = lax.while_loop(step_cond, step_body,
                                  (hix, lo, jnp.zeros((QB, 1), F32), jnp.float32(QB)))
    need = k_eff - count(thr, True)
    upper = jnp.where(lax.broadcasted_iota(jnp.int32, (QB, QB), 0) <= lax.broadcasted_iota(jnp.int32, (QB, QB), 1),
                      1.0, 0.0).astype(BF16)

    carry = jnp.zeros((QB, 1), F32)
    for c in range(NQB):
        s = sc_scr[:, c * QB:(c + 1) * QB]
        eq = s == thr
        eqf = jnp.where(eq, 1.0, 0.0)
        prefix = _dot(eqf.astype(BF16), upper) + carry
        mk_scr[:, c * QB:(c + 1) * QB] = jnp.where((s > thr) | (eq & (prefix <= need)), 1.0, 0.0)
        carry = carry + jnp.sum(eqf, axis=-1, keepdims=True)

    def logits(g, rg, raw):
        bias = _region_bias(bnear_ref, i, rg, slice(4 * g, 4 * g + 4))
        return _mask_rows4(mk_scr[:, rg * RG:(rg + 1) * RG] > 0.5, raw + bias)

    streams = [((_stack_heads(q_ref, g) * Q_SCALE * LOG2E).astype(BF16), kv_ref, 2 * g * HD, (2 * g + 1) * HD,
                functools.partial(logits, g)) for g in range(2)]
    outs = _attend_regions(i, streams, (m_scr, acc_scr))
    for g in range(2):
        for a in range(4):
            o_ref[:, (4 * g + a) * HD:(4 * g + a + 1) * HD] = outs[g][a * QB:(a + 1) * QB]


def _dsa_prompt(z, zb, bnear):
    return pl.pallas_call(
        _dsa_prompt_kernel,
        grid=(BATCH, NQB),
        in_specs=[
            pl.BlockSpec((QB, 1024), lambda b, i: (b * NQB + i, E_QB // 1024)),
            pl.BlockSpec((QB, 256), lambda b, i: (b * NQB + i, E_QI // 256)),
            pl.BlockSpec((QB, 128), lambda b, i: (b * NQB + i, E_SM // 128)),
            pl.BlockSpec((SEQ, 128), lambda b, i: (b, E_SM // 128)),
            pl.BlockSpec((SEQ, 512), lambda b, i: (b, E_KVB // 512)),
            pl.BlockSpec((3, 8, QB, QB), lambda b, i: (0, 0, 0, 0)),
        ],
        out_specs=pl.BlockSpec((QB, 1024), lambda b, i: (b * NQB + i, 0)),
        out_shape=jax.ShapeDtypeStruct((N_TOK_P, 1024), F32),
        scratch_shapes=[pltpu.VMEM((QB, SEQ), F32), pltpu.VMEM((QB, SEQ), F32)] + _softmax_state(2, 4 * QB),
        compiler_params=_cparams(("parallel", "arbitrary")),
        name="dsa_prompt",
    )(z, z, z, z, zb, bnear)


DIL_CFG = ((128, 1), (512, 4), (2048, 16))
DIL_REL = 5


def _dil_count(dist):
    cnt = jnp.zeros(dist.shape, F32)
    for window, dil in DIL_CFG:
        cnt = cnt + jnp.where((dist >= 0) & (dist <= window) & ((dist & (dil - 1)) == 0), 1.0, 0.0)
    return cnt


def _dil_prompt_kernel(q_ref, kva_ref, kvb_ref, bnear_ref, o_ref, tab_scr, m_scr, acc_scr):
    i = pl.program_id(1)

    @pl.when(i == 0)
    def _():
        for rel in range(DIL_REL + 2):
            if rel <= DIL_REL:
                cnt = _dil_count(_rel_dist(rel, 0))
                log_n = jnp.where(cnt > 2.5, math.log2(3.0), jnp.where(cnt > 1.5, 1.0,
                                                                       jnp.where(cnt > 0.5, 0.0, NEG_INF)))
            else:
                log_n = jnp.full((QB, QB), NEG_INF, F32)
            for h in range(8):
                tab_scr[rel, h] = bnear_ref[min(rel, 2), h] + log_n

    def logits(h, rg, raw):
        tiles = []
        for c in range(RG_CHUNKS):
            rel = i - (rg * RG_CHUNKS + c)
            tiles.append(tab_scr[jnp.where(rel < 0, DIL_REL + 1, jnp.minimum(rel, DIL_REL)), h])
        return raw + jnp.concatenate(tiles, axis=1)

    streams = []
    for h in range(8):
        col = (h % 4) * 2 * HD
        q_bf = (q_ref[:, h * HD:(h + 1) * HD] * Q_SCALE * LOG2E).astype(BF16)
        streams.append((q_bf, kva_ref if h < 4 else kvb_ref, col, col + HD, functools.partial(logits, h)))
    outs = _attend_regions(i, streams, (m_scr, acc_scr), qk_first=True)
    for h in range(8):
        o_ref[:, h * HD:(h + 1) * HD] = outs[h]


def _dil_prompt(z, zb, bnear):
    return pl.pallas_call(
        _dil_prompt_kernel,
        grid=(BATCH, NQB),
        in_specs=[
            pl.BlockSpec((QB, 1024), lambda b, i: (b * NQB + i, O_QC // 1024)),
            pl.BlockSpec((SEQ, 1024), lambda b, i: (b, O_KVC // 1024)),
            pl.BlockSpec((SEQ, 1024), lambda b, i: (b, O_KVC // 1024 + 1)),
            pl.BlockSpec((3, 8, QB, QB), lambda b, i: (0, 0, 0, 0)),
        ],
        out_specs=pl.BlockSpec((QB, 1024), lambda b, i: (b * NQB + i, 0)),
        out_shape=jax.ShapeDtypeStruct((N_TOK_P, 1024), F32),
        scratch_shapes=[pltpu.VMEM((DIL_REL + 2, 8, QB, QB), F32)] + _softmax_state(8, QB),
        compiler_params=_cparams(("arbitrary", "arbitrary")),
        name="dilated_prompt",
    )(z, zb, zb, bnear)


def _rope_rows(x, cos2, sin2):
    return x * cos2 + pltpu.roll(x, 64, 1) * sin2


def _readout(o, gate, gn):
    mu = jnp.mean(o, axis=-1, keepdims=True)
    d = o - mu
    var = jnp.mean(d * d, axis=-1, keepdims=True)
    return gate * _sigmoid(gate) * (d * lax.rsqrt(var + RMS_EPS)) * gn


def _ret_prompt_kernel(q_ref, k_ref, v_ref, gr_ref, cos_ref, sin_ref, gn_ref, o_ref, st_ref, st_scr):
    ci = pl.program_id(1)

    @pl.when(ci == 0)
    def _():
        st_scr[...] = jnp.zeros_like(st_scr)

    cos2, sin2 = cos_ref[...], sin_ref[...]
    r = lax.broadcasted_iota(jnp.int32, (QB, QB), 0)
    c = lax.broadcasted_iota(jnp.int32, (QB, QB), 1)
    rel = (r - c).astype(F32)
    pos_col = (lax.broadcasted_iota(jnp.int32, (QB, 1), 0)).astype(F32)
    for h in range(R_HEADS):
        lg = LOG_G[h]
        q = _rope_rows(q_ref[:, h * R_DK:(h + 1) * R_DK], cos2, sin2)
        k = _rope_rows(k_ref[:, h * R_DK:(h + 1) * R_DK], cos2, sin2) * RK_SCALE
        v = v_ref[:, h * R_DV:(h + 1) * R_DV]
        state = st_scr[h]
        decay = jnp.where(r >= c, jnp.exp(lg * jnp.maximum(rel, 0.0)), 0.0)
        q_bf, v_bf = q.astype(BF16), v.astype(BF16)
        scores = _dot_nt(q_bf, k.astype(BF16)) * decay
        out = _dot(scores.astype(BF16), v_bf)
        out = out + _dot(q_bf, state.astype(BF16)) * jnp.exp(lg * (pos_col + 1.0))
        zeta = jnp.exp(lg * (QB - 1.0 - pos_col))
        new_state = state * math.exp(lg * QB) + _dot((k * zeta).T.astype(BF16), v_bf)
        st_scr[h] = new_state
        st_ref[h] = new_state
        o_ref[:, h * R_DV:(h + 1) * R_DV] = _readout(out, gr_ref[:, h * R_DV:(h + 1) * R_DV],
                                                     gn_ref[:, h * R_DV:(h + 1) * R_DV])


def _ret_prompt(z, cos2, sin2, gn):
    return pl.pallas_call(
        _ret_prompt_kernel,
        grid=(BATCH, NQB),
        in_specs=[
            pl.BlockSpec((QB, 512), lambda b, i: (b * NQB + i, O_QR // 512)),
            pl.BlockSpec((QB, 512), lambda b, i: (b * NQB + i, O_KR // 512)),
            pl.BlockSpec((QB, 1024), lambda b, i: (b * NQB + i, O_VR // 1024)),
            pl.BlockSpec((QB, 1024), lambda b, i: (b * NQB + i, O_GR // 1024)),
            pl.BlockSpec((QB, 128), lambda b, i: (i, 0)),
            pl.BlockSpec((QB, 128), lambda b, i: (i, 0)),
            pl.BlockSpec((1, 1024), lambda b, i: (0, 0)),
        ],
        out_specs=[pl.BlockSpec((QB, 1024), lambda b, i: (b * NQB + i, 0)),
                   pl.BlockSpec((None, R_HEADS, R_DK, R_DV), lambda b, i: (b, 0, 0, 0))],
        out_shape=[jax.ShapeDtypeStruct((N_TOK_P, 1024), F32),
                   jax.ShapeDtypeStruct((BATCH, R_HEADS, R_DK, R_DV), F32)],
        scratch_shapes=[pltpu.VMEM((R_HEADS, R_DK, R_DV), F32)],
        compiler_params=_cparams(("parallel", "arbitrary")),
        name="retention_prompt",
    )(z, z, z, z, cos2, sin2, gn.reshape(1, 1024))


DI_QA, DI_QB, DI_KSEL, DI_VSEL, DI_KWIN, DI_VWIN, DI_KB, DI_VB, DI_QI, DI_KI, DI_MISC = (
    0, 128, 256, 384, 512, 640, 768, 896, 1024, 1088, 1152)
DEC_IN_W = 1280
DEC_SEL_BLOCKS = PAST_LEN // SEL_BLOCK + 1
DEC_SPLIT, DEC_ROUNDS = 16, 4


def _kv_rows(ref, start, kv):
    return ref[pl.ds(4 * start + kv, 2 * PAGE, stride=2), :].astype(BF16)


def _decode_attend(q, q_bf, chunks, k_new, v_new, bias_new, new_ok, logit_fn):
    row = lax.broadcasted_iota(jnp.int32, (8, 2 * PAGE), 0)
    col = lax.broadcasted_iota(jnp.int32, (8, 2 * PAGE), 1)
    own = (col & 1) == jnp.where(row < 4, 0, 1)
    logits = []
    for k, (ref, start) in enumerate(chunks):
        logits.append(jnp.where(own, logit_fn(k, _dot_nt(q_bf, _kv_rows(ref, start, 0))), NEG_INF))
    s_new = jnp.sum(q * k_new, axis=-1, keepdims=True) + bias_new
    if new_ok is not None:
        s_new = jnp.where(new_ok, s_new, NEG_INF)
    m = s_new
    for s in logits:
        m = jnp.maximum(m, jnp.max(s, axis=-1, keepdims=True))
    p_new = jnp.exp(s_new - m)
    l = p_new
    acc = p_new * v_new
    for s, (ref, start) in zip(logits, chunks):
        p = jnp.exp(s - m)
        l = l + jnp.sum(p, axis=-1, keepdims=True)
        acc = acc + _dot(p.astype(BF16), _kv_rows(ref, start, 1))
    return acc / jnp.maximum(l, 1e-30)


def _even_sample_kernel(pt_ref, *refs):
    sel_pages = refs[0:N_PAGES]
    kvb_pages = refs[N_PAGES:2 * N_PAGES]
    idx_pages = refs[2 * N_PAGES:3 * N_PAGES]
    din_ref, cmp_ref, win_ref, expand_ref, dec_ref, decc_ref, o_ref = refs[3 * N_PAGES:]
    d = din_ref[...]
    row = lax.broadcasted_iota(jnp.int32, (8, QB), 0)
    row_c = lax.broadcasted_iota(jnp.int32, (8, N_CMP), 0)
    pick_c = lambda a, b: jnp.where(row_c < 4, a, b)
    pick = lambda a, b: jnp.where(row < 4, a, b)
    dec_a, dec_b = dec_ref[0:8, :], dec_ref[8:16, :]

    qa = d[:, DI_QA:DI_QA + HD] * Q_SCALE
    qa_bf = qa.astype(BF16)
    cmp = cmp_ref[...]
    logit = pick_c(_dot_nt3(qa, cmp[:, 0:HD]), _dot_nt3(qa, cmp[:, 2 * HD:3 * HD])) + decc_ref[...]
    e = jnp.exp(logit - jnp.max(logit, axis=-1, keepdims=True))
    p_c = e / jnp.maximum(jnp.sum(e, axis=-1, keepdims=True), 1e-30)
    p_bf = p_c.astype(BF16)
    cmp_bf = cmp.astype(BF16)
    o_c = pick(_dot(p_bf, cmp_bf[:, HD:2 * HD]), _dot(p_bf, cmp_bf[:, 3 * HD:4 * HD]))
    g0 = jnp.sum(jnp.where(row_c < 4, p_c, 0.0), axis=0, keepdims=True)
    g1 = jnp.sum(jnp.where(row_c < 4, 0.0, p_c), axis=0, keepdims=True)
    p_g = pick_c(jnp.broadcast_to(g0, (8, N_CMP)), jnp.broadcast_to(g1, (8, N_CMP)))
    blk = lax.broadcasted_iota(jnp.int32, (8, N_CMP), 1)
    score = jnp.concatenate([p_g[:, :N_CMP // 2] + p_g[:, N_CMP // 2:], jnp.zeros((8, N_CMP // 2), F32)], axis=1)
    cur = PAST_LEN // SEL_BLOCK
    score = jnp.where((blk == 0) | (blk == cur) | (blk == cur - 1), FORCE_SCORE, score)
    score = jnp.where(blk < DEC_SEL_BLOCKS, score, -BIG)
    sel = _rank_select(score, DEC_SEL_BLOCKS, N_SEL) * jnp.where(blk < DEC_SEL_BLOCKS, 1.0, 0.0)
    sel_pos = _dot(sel.astype(BF16), expand_ref[...])
    cols = lambda k: slice(2 * k * PAGE, 2 * (k + 1) * PAGE)
    new_col = slice(2 * PAST_LEN, 2 * PAST_LEN + 1)

    def sel_logits(k, raw):
        return jnp.where(sel_pos[:, cols(k)] > 0.5, raw + dec_a[:, cols(k)], NEG_INF)

    o_s = _decode_attend(qa, qa_bf, [(pg, 0) for pg in sel_pages], d[:, DI_KSEL:DI_KSEL + HD],
                         d[:, DI_VSEL:DI_VSEL + HD], dec_a[:, new_col], None, sel_logits)
    win0 = (PAST_LEN - A_WINDOW) // PAGE

    def win_logits(k, raw):
        return raw + dec_a[:, cols(win0 + k)]

    o_w = _decode_attend(qa, qa_bf, [(win_ref, k * PAGE) for k in range(A_WINDOW // PAGE)],
                         d[:, DI_KWIN:DI_KWIN + HD], d[:, DI_VWIN:DI_VWIN + HD], dec_a[:, new_col], None, win_logits)
    gate = _sigmoid(d[:, DI_MISC:DI_MISC + 3])
    o_ref[:, 0:HD] = gate[:, 0:1] * o_c + gate[:, 1:2] * o_s + gate[:, 2:3] * o_w

    qi = d[:, DI_QI:DI_QI + 64]
    wi = d[:, DI_MISC + 3:DI_MISC + 4] * 0.5
    row16 = lax.broadcasted_iota(jnp.int32, (N_PAGES, PAGE), 0)
    sc = jnp.zeros((N_PAGES, PAGE), F32)
    for k, pg in enumerate(idx_pages):
        sc_k = jnp.sum(wi * jnp.maximum(_dot_nt3(qi, pg[...]), 0.0), axis=0, keepdims=True)
        sc = jnp.where(row16 == k, sc_k, sc)
    sc_new = jnp.sum(wi * jnp.maximum(jnp.sum(qi * d[:, DI_KI:DI_KI + 64], axis=-1, keepdims=True), 0.0),
                     axis=0, keepdims=True)
    total = lambda x: jnp.sum(jnp.sum(x, axis=-1, keepdims=True), axis=0, keepdims=True)

    def count(thr, strict):
        hit = (lambda x: x > thr) if strict else (lambda x: x >= thr)
        return total(jnp.where(hit(sc), 1.0, 0.0)) + jnp.where(hit(sc_new), 1.0, 0.0)

    def max_below(bound):
        best = jnp.max(jnp.max(jnp.where(sc < bound, sc, -BIG), axis=-1, keepdims=True), axis=0, keepdims=True)
        return jnp.maximum(best, jnp.where(sc_new < bound, sc_new, -BIG))

    k_top = float(IDX_TOPK)
    lo = jnp.minimum(jnp.min(jnp.min(sc, axis=-1, keepdims=True), axis=0, keepdims=True), sc_new)
    hib = jnp.maximum(jnp.max(jnp.max(sc, axis=-1, keepdims=True), axis=0, keepdims=True), sc_new)
    hix = jnp.full((1, 1), BIG, F32)
    frac = (lax.broadcasted_iota(jnp.int32, (DEC_SPLIT, 1, 1), 0) + 1).astype(F32) * (1.0 / DEC_SPLIT)
    for _ in range(DEC_ROUNDS):
        tk = lo + (hib - lo) * frac
        cnt = jnp.sum(jnp.sum(jnp.where(sc[None] >= tk, 1.0, 0.0), axis=2, keepdims=True), axis=1, keepdims=True)
        ge = cnt + jnp.where(sc_new >= tk, 1.0, 0.0) >= k_top
        fail = jnp.min(jnp.where(ge, BIG, tk), axis=0)
        lo = jnp.max(jnp.where(ge, tk, lo), axis=0)
        hib = jnp.minimum(hib, fail)
        hix = jnp.minimum(hix, fail)

    def step_body(c):
        bound, thr, _ = c
        cand = max_below(bound)
        ok = count(cand, False) >= k_top
        return jnp.where(ok, bound, cand), jnp.where(ok, cand, thr), jnp.sum(jnp.where(ok, 0.0, 1.0))

    _, thr, _ = lax.while_loop(lambda c: c[2] > 0.5, step_body, (hix, lo, jnp.float32(1.0)))
    n_gt = count(thr, True)
    need = k_top - n_gt
    eq = jnp.where(sc == thr, 1.0, 0.0).astype(BF16)
    r128 = lax.broadcasted_iota(jnp.int32, (QB, QB), 0)
    c128 = lax.broadcasted_iota(jnp.int32, (QB, QB), 1)
    upper = jnp.where(r128 <= c128, 1.0, 0.0).astype(BF16)
    r16 = lax.broadcasted_iota(jnp.int32, (N_PAGES, N_PAGES), 0)
    c16 = lax.broadcasted_iota(jnp.int32, (N_PAGES, N_PAGES), 1)
    lower = jnp.where(c16 < r16, 1.0, 0.0).astype(BF16)
    prefix = _dot(eq, upper) + jnp.sum(_dot(lower, eq), axis=-1, keepdims=True)
    keep = jnp.where((sc > thr) | ((sc == thr) & (prefix <= need)), 1.0, 0.0)
    n_eq_past = total(eq.astype(F32))
    new_ok = (sc_new > thr) | ((sc_new == thr) & (n_gt + n_eq_past + 1.0 <= k_top))
    pos = lax.broadcasted_iota(jnp.int32, (PAGE, 2 * PAGE), 0)
    col2 = lax.broadcasted_iota(jnp.int32, (PAGE, 2 * PAGE), 1)
    keep2 = _dot(keep.astype(BF16), jnp.where(col2 // 2 == pos, 1.0, 0.0).astype(BF16))

    qb = d[:, DI_QB:DI_QB + HD] * Q_SCALE

    def b_logits(k, raw):
        return jnp.where(keep2[k:k + 1, :] > 0.5, raw + dec_b[:, cols(k)], NEG_INF)

    o_ref[:, HD:2 * HD] = _decode_attend(
        qb, qb.astype(BF16), [(pg, 0) for pg in kvb_pages], d[:, DI_KB:DI_KB + HD], d[:, DI_VB:DI_VB + HD],
        dec_b[:, new_col], new_ok, b_logits)


def _seq_page_map(k, s, pt_ref):
    return (pt_ref[s, k], 0, 0)


def _even_sample(page_table, pool_sel, pool_b, pool_idx, din, cmp_sum, win_buf, expand, dec, decc):
    specs = []
    for shape in ((4 * PAGE, HD), (4 * PAGE, HD), (PAGE, 64)):
        specs += [pl.BlockSpec((None,) + shape, functools.partial(_seq_page_map, k)) for k in range(N_PAGES)]
    specs += [
        pl.BlockSpec((None, 8, DEC_IN_W), lambda s, pt: (s, 0, 0)),
        pl.BlockSpec((None, N_CMP, 512), lambda s, pt: (s, 0, 0)),
        pl.BlockSpec((None, 4 * A_WINDOW, HD), lambda s, pt: (s, 0, 0)),
        pl.BlockSpec((N_CMP, 2 * PAST_LEN), lambda s, pt: (0, 0)),
        pl.BlockSpec((16, DEC_W), lambda s, pt: (0, 0)),
        pl.BlockSpec((8, N_CMP), lambda s, pt: (0, 0)),
    ]
    grid_spec = pltpu.PrefetchScalarGridSpec(
        num_scalar_prefetch=1, grid=(DEC_BATCH,), in_specs=specs,
        out_specs=pl.BlockSpec((None, 8, 2 * HD), lambda s, pt: (s, 0, 0)))
    return pl.pallas_call(
        _even_sample_kernel,
        grid_spec=grid_spec,
        out_shape=jax.ShapeDtypeStruct((DEC_BATCH, 8, 2 * HD), F32),
        compiler_params=_cparams(("parallel",)),
        name="even_sample",
    )(page_table, *([pool_sel] * N_PAGES), *([pool_b] * N_PAGES), *([pool_idx] * N_PAGES),
      din, cmp_sum, win_buf, expand, dec, decc)


def _odd_sample_kernel(c0_ref, c1_ref, c2_ref, qkv_ref, rrow_ref, kcol_ref, st_ref, bdil_ref, tbl_ref,
                       rope_ref, ropec_ref, gn_ref, oc_ref, od_ref, st_out_ref):
    qkv = qkv_ref[...]
    q = qkv[:, 0:HD] * Q_SCALE
    lane = lax.broadcasted_iota(jnp.int32, (QB, 8), 1)
    lane1 = lax.broadcasted_iota(jnp.int32, (1, 8), 1)
    cfg_refs = (c0_ref, c1_ref, c2_ref)
    logits = []
    for cfg in range(3):
        s = bdil_ref[cfg]
        for h in range(8):
            s_h = jnp.sum(cfg_refs[cfg][:, 2 * h, :] * q[h:h + 1, :], axis=-1, keepdims=True)
            s = s + jnp.where(lane == h, s_h, 0.0)
        logits.append(s)
    s_new = tbl_ref[0:1, 0:8]
    for h in range(8):
        s_h = jnp.sum(q[h:h + 1, :] * qkv[h:h + 1, HD:2 * HD], axis=-1, keepdims=True)
        s_new = s_new + jnp.where(lane1 == h, s_h, 0.0)
    m = s_new
    for s in logits:
        m = jnp.maximum(m, jnp.max(s, axis=0, keepdims=True))
    p_new = 3.0 * jnp.exp(s_new - m)
    probs = [jnp.exp(s - m) for s in logits]
    l = p_new
    for p in probs:
        l = l + jnp.sum(p, axis=0, keepdims=True)
    inv = 1.0 / jnp.maximum(l, 1e-30)
    for h in range(8):
        acc = p_new[:, h:h + 1] * qkv[h:h + 1, 2 * HD:3 * HD]
        for cfg in range(3):
            v = cfg_refs[cfg][:, 2 * h + 1, :]
            acc = acc + jnp.sum(probs[cfg][:, h:h + 1] * v, axis=0, keepdims=True)
        oc_ref[:, h * HD:(h + 1) * HD] = acc * inv[:, h:h + 1]

    rrow = rrow_ref[...]
    cos2, sin2 = rope_ref[0:1, :], rope_ref[1:2, :]
    qr = _rope_rows(rrow[:, 0:R_DK], cos2, sin2)
    kr = _rope_rows(rrow[:, R_DK:2 * R_DK], cos2, sin2) * RK_SCALE
    qk = jnp.sum(qr * kr, axis=-1, keepdims=True)
    kc = kcol_ref[...]
    kc = (kc * ropec_ref[:, 0:1] + jnp.concatenate([kc[64:], kc[:64]], axis=0) * ropec_ref[:, 1:2]) * RK_SCALE
    qr_bf = jnp.concatenate([qr, jnp.zeros_like(qr)], axis=0).astype(BF16)
    row8 = lax.broadcasted_iota(jnp.int32, (8, R_DV), 0)
    for h in range(R_HEADS):
        gamma = math.exp(LOG_G[h])
        state = st_ref[h]
        v = rrow[h:h + 1, 2 * R_DK:2 * R_DK + R_DV]
        q_state = jnp.sum(jnp.where(row8 == h, _dot(qr_bf, state.astype(BF16)), 0.0), axis=0, keepdims=True)
        out = qk[h:h + 1, :] * v + q_state * gamma
        st_out_ref[h] = state * gamma + kc[:, h:h + 1] * v
        od_ref[:, h * R_DV:(h + 1) * R_DV] = _readout(out, rrow[h:h + 1, 2 * R_DK + R_DV:2 * R_DK + 2 * R_DV],
                                                      gn_ref[:, h * R_DV:(h + 1) * R_DV])


def _odd_sample(c_buf, qkv, rrow, kcol, state, bdil, rel_bias, rope_row, rope_col, gn):
    views = [c_buf.reshape(DEC_BATCH, PAST_LEN // dil, dil * 16, HD) for _, dil in DIL_CFG]
    c_specs = [pl.BlockSpec((None, QB, 16, HD), functools.partial(lambda nb, s: (s, nb, 0, 0), PAST_LEN // dil // QB - 1))
               for _, dil in DIL_CFG]
    return pl.pallas_call(
        _odd_sample_kernel,
        grid=(DEC_BATCH,),
        in_specs=c_specs + [
            pl.BlockSpec((None, 8, 3 * HD), lambda s: (s, 0, 0)),
            pl.BlockSpec((None, R_HEADS, 2 * R_DK + 2 * R_DV), lambda s: (s, 0, 0)),
            pl.BlockSpec((None, R_DK, R_HEADS), lambda s: (s, 0, 0)),
            pl.BlockSpec((None, R_HEADS, R_DK, R_DV), lambda s: (s, 0, 0, 0)),
            pl.BlockSpec((3, QB, 8), lambda s: (0, 0, 0)),
            pl.BlockSpec((N_BUCKETS, 16), lambda s: (0, 0)),
            pl.BlockSpec((2, R_DK), lambda s: (0, 0)),
            pl.BlockSpec((R_DK, 2), lambda s: (0, 0)),
            pl.BlockSpec((1, 1024), lambda s: (0, 0)),
        ],
        out_specs=[pl.BlockSpec((None, 1, 1024), lambda s: (s, 0, 0)),
                   pl.BlockSpec((None, 1, 1024), lambda s: (s, 0, 0)),
                   pl.BlockSpec((None, R_HEADS, R_DK, R_DV), lambda s: (s, 0, 0, 0))],
        out_shape=[jax.ShapeDtypeStruct((DEC_BATCH, 1, 1024), F32),
                   jax.ShapeDtypeStruct((DEC_BATCH, 1, 1024), F32),
                   jax.ShapeDtypeStruct((DEC_BATCH, R_HEADS, R_DK, R_DV), F32)],
        compiler_params=_cparams(("parallel",)),
        name="odd_sample",
    )(*views, qkv, rrow, kcol, state, bdil, rel_bias, rope_row, rope_col, gn.reshape(1, 1024))


def _rope_tables(pos):
    half = R_DK // 2
    inv = 10000.0 ** (-jnp.arange(half, dtype=F32) / half)
    ang = pos.astype(F32)[:, None] * inv[None, :]
    cos, sin = jnp.cos(ang), jnp.sin(ang)
    return jnp.concatenate([cos, cos], axis=-1), jnp.concatenate([-sin, sin], axis=-1)


def _permute_even_w(w):
    qa, kva, ga, qb, kvb, qi, ki, wi = jnp.split(w, [1024, 2560, 2584, 3608, 4120, 4376, 4440], axis=1)
    pad = jnp.zeros((w.shape[0], EVEN_W - w.shape[1]), w.dtype)
    return jnp.concatenate([qa, qb, kva, kvb, qi, ki, wi, ga, pad], axis=1)


def _even_decode(zs, e, bias, page_table, cache_a_cmp, cache_a_sel, cache_b_kv, cache_b_idx, state_a_win,
                 cmp_pe, cmp_w_bf):
    dec, decc = bias[2], bias[3]
    n_pool = cache_a_cmp.shape[1]
    cmp_s = _even_first(_compress_sample(cache_a_cmp[e].reshape(n_pool, 4 * PAGE, HD), page_table, cmp_pe, cmp_w_bf),
                        DEC_BATCH)
    heads = lambda a: a.reshape(DEC_BATCH, 8, HD)
    rep = lambda a: jnp.repeat(a.reshape(DEC_BATCH, 2, HD), 4, axis=1)
    kva = zs[:, E_CMP:E_KVB].reshape(DEC_BATCH, 3, 2, 2, HD)
    kvb = zs[:, E_KVB:E_QI].reshape(DEC_BATCH, 2, 2, HD)
    sm = zs[:, E_SM:E_SM + 128]
    zero4 = lambda a: jnp.concatenate([a, jnp.zeros_like(a)], axis=1)
    qi8 = zero4(zs[:, E_QI:E_SM].reshape(DEC_BATCH, 4, 64))
    ki8 = jnp.broadcast_to(sm[:, None, SM_KI:SM_KI + 64], (DEC_BATCH, 8, 64))
    gates = sm[:, SM_GA:SM_GA + 24].reshape(DEC_BATCH, 8, 3)
    wi8 = zero4(sm[:, SM_WI:SM_WI + 4].reshape(DEC_BATCH, 4, 1))
    misc = jnp.concatenate([gates, wi8, jnp.zeros((DEC_BATCH, 8, DEC_IN_W - DI_MISC - 4), F32)], axis=2)
    din = jnp.concatenate([
        heads(zs[:, E_QA:E_QB]), heads(zs[:, E_QB:E_CMP]),
        rep(kva[:, 1, :, 0]), rep(kva[:, 1, :, 1]), rep(kva[:, 2, :, 0]), rep(kva[:, 2, :, 1]),
        rep(kvb[:, :, 0]), rep(kvb[:, :, 1]), qi8, ki8, misc], axis=2)
    expand = (jnp.arange(2 * PAST_LEN)[None, :] // (2 * SEL_BLOCK) == jnp.arange(N_CMP)[:, None]).astype(BF16)
    o_s = _even_sample(page_table, cache_a_sel[e].reshape(n_pool, 4 * PAGE, HD),
                       cache_b_kv[e].reshape(n_pool, 4 * PAGE, HD), cache_b_idx[e],
                       din, cmp_s, state_a_win[e].reshape(DEC_BATCH, 4 * A_WINDOW, HD), expand, dec, decc)
    return o_s[:, :, :HD].reshape(DEC_BATCH, 1024), o_s[:, :, HD:].reshape(DEC_BATCH, 1024)


def _even_layer(x, e, bias, page_table, cache_a_cmp, cache_a_sel, cache_b_kv, cache_b_idx, state_a_win,
                g, w_in, w_out, cmp_pe, cmp_w):
    bnear, bcmp = bias[0], bias[1]
    z, zb = _proj_in(x, g, _permute_even_w(w_in).astype(BF16), 896)
    cmp_w_bf = cmp_w.astype(BF16)
    cmp_p = _even_first(_compress_prompt(z, cmp_pe, cmp_w_bf), BATCH)
    oa_p = _nsa_prompt(z, zb, cmp_p, bnear[:, 0:8], bcmp)
    ob_p = _dsa_prompt(z, zb, bnear[:, 8:16])
    zs = z[N_TOK_P:]
    oa_s, ob_s = _even_decode(zs, e, bias, page_table, cache_a_cmp, cache_a_sel, cache_b_kv, cache_b_idx,
                              state_a_win, cmp_pe, cmp_w_bf)
    oa = jnp.concatenate([oa_p, oa_s], axis=0)
    ob = jnp.concatenate([ob_p, ob_s], axis=0)
    x = _proj_out(x, oa, ob, w_out[:1024].astype(BF16), w_out[1024:].astype(BF16))

    kv5 = lambda a, n: a.reshape(n, -1, 2, 2, HD)
    zp = lambda lo, hi: z[:N_TOK_P, lo:hi]
    state = (
        kv5(zp(E_CMP, E_SEL), BATCH), kv5(zs[:, E_CMP:E_SEL], DEC_BATCH),
        kv5(zp(E_SEL, E_WIN), BATCH), kv5(zs[:, E_SEL:E_WIN], DEC_BATCH),
        kv5(zp(E_KVB, E_QI), BATCH), kv5(zs[:, E_KVB:E_QI], DEC_BATCH),
        zp(E_SM, E_SM + 64).reshape(BATCH, SEQ, 64), zs[:, None, E_SM:E_SM + 64],
        kv5(zp(E_WIN, E_KVB).reshape(BATCH, SEQ, 512)[:, SEQ - A_WINDOW:], BATCH), kv5(zs[:, E_WIN:E_KVB], DEC_BATCH),
    )
    return x, state


def _odd_layer(x, o, bias, state_c_win, state_ret, g, w_in, w_out, gn):
    bnear, bdil = bias[0], bias[4]
    z, zb = _proj_in(x, g, w_in.astype(BF16), 1536)
    cos_p, sin_p = _rope_tables(jnp.arange(SEQ))
    oc_p = _dil_prompt(z, zb, bnear[:, 0:8])
    od_p, ret_p = _ret_prompt(z, cos_p, sin_p, gn)

    zs = z[N_TOK_P:]
    kvc = zs[:, O_KVC:O_QR].reshape(DEC_BATCH, 8, 2, HD)
    qkv = jnp.concatenate([zs[:, O_QC:O_KVC].reshape(DEC_BATCH, 8, HD), kvc[:, :, 0], kvc[:, :, 1]], axis=2)
    rrow = jnp.concatenate([zs[:, O_QR:O_KR].reshape(DEC_BATCH, R_HEADS, R_DK),
                            zs[:, O_KR:O_VR].reshape(DEC_BATCH, R_HEADS, R_DK),
                            zs[:, O_VR:O_GR].reshape(DEC_BATCH, R_HEADS, R_DV),
                            zs[:, O_GR:].reshape(DEC_BATCH, R_HEADS, R_DV)], axis=2)
    kcol = jnp.swapaxes(zs[:, O_KR:O_VR].reshape(DEC_BATCH, R_HEADS, R_DK), 1, 2)
    cos_s, sin_s = _rope_tables(jnp.full((1,), PAST_LEN))
    rope_row = jnp.concatenate([cos_s, sin_s], axis=0)
    oc_s, od_s, ret_s = _odd_sample(state_c_win[o], qkv, rrow, kcol,
                                    state_ret[o], bdil, bias[5], rope_row, rope_row.T, gn)
    oc = jnp.concatenate([oc_p, oc_s.reshape(DEC_BATCH, 1024)], axis=0)
    od = jnp.concatenate([od_p, od_s.reshape(DEC_BATCH, 1024)], axis=0)
    x = _proj_out(x, oc, od, w_out[:1024].astype(BF16), w_out[1024:].astype(BF16))
    state = (z[:N_TOK_P, O_KVC:O_QR].reshape(BATCH, SEQ, 8, 2, HD), kvc[:, None], ret_p, ret_s)
    return x, state


def kernel(x_prompt, x_sample, cache_a_cmp, cache_a_sel, cache_b_kv, cache_b_idx, state_a_win, state_c_win,
           state_ret, page_table, p_prompt, p_sample, rel_bias, norm_g, final_norm, ffn_w_in, ffn_w_out,
           ple_gate, ple_proj, even_w_in, even_w_out, nsa_cmp_pe, nsa_cmp_w, odd_w_in, odd_w_out, ret_gn):
    depth = norm_g.shape[0]
    x = jnp.concatenate([x_prompt.reshape(N_TOK_P, D_MODEL), x_sample.reshape(DEC_BATCH, D_MODEL)], axis=0)
    bias = tuple(_bias_tables(rel_bias)) + (rel_bias,)
    w_in_bf = ffn_w_in.astype(BF16)
    w_out_bf = ffn_w_out.astype(BF16)
    even_states, odd_states = [], []
    for i in range(depth):
        x = _ffn_half(x, norm_g[i, 0], w_in_bf, w_out_bf, i, 0)
        if i % 2 == 0:
            e = i // 2
            x, st = _even_layer(x, e, bias, page_table, cache_a_cmp, cache_a_sel, cache_b_kv, cache_b_idx,
                                state_a_win, norm_g[i, 1], even_w_in[e], even_w_out[e], nsa_cmp_pe[e], nsa_cmp_w[e])
            even_states.append(st)
        else:
            o = i // 2
            x, st = _odd_layer(x, o, bias, state_c_win, state_ret, norm_g[i, 1], odd_w_in[o], odd_w_out[o],
                               ret_gn[o])
            odd_states.append(st)
        x = _ffn_half(x, norm_g[i, 2], w_in_bf, w_out_bf, i, 1)
        p = jnp.concatenate([p_prompt[i].reshape(N_TOK_P, PLE_DIM), p_sample[i].reshape(DEC_BATCH, PLE_DIM)], axis=0)
        x = _ple(x, p, norm_g[i, 3], ple_gate[i].astype(BF16), ple_proj[i].astype(BF16), final_norm,
                 final=(i == depth - 1))
    y_prompt = x[:N_TOK_P].reshape(BATCH, SEQ, D_MODEL)
    y_sample = x[N_TOK_P:].reshape(DEC_BATCH, 1, D_MODEL)
    ev = [jnp.stack([st[k] for st in even_states]) for k in range(10)]
    od = [jnp.stack([st[k] for st in odd_states]) for k in range(4)]
    return (y_prompt, y_sample, *ev, *od)
```

```python
import functools
import math

import numpy as np
import jax
import jax.numpy as jnp
from jax import lax
from jax.experimental import pallas as pl
from jax.experimental.pallas import tpu as pltpu

F32 = jnp.float32
BF16 = jnp.bfloat16
HI = lax.Precision.HIGHEST

D_MODEL = 2048
BATCH = 4
SEQ = 2048
DEC_BATCH = 128
PAST_LEN = 2048
PAGE = 128
N_PAGES = PAST_LEN // PAGE
HD = 128
D_FF = 5632
PLE_DIM = 256
N_TOK_P = BATCH * SEQ
N_TOK = N_TOK_P + DEC_BATCH
QB = 128
NQB = SEQ // QB
CMP_BLOCK = 32
N_CMP = SEQ // CMP_BLOCK
SEL_BLOCK = 64
N_SEL = 16
A_WINDOW = 512
IDX_TOPK = 256
R_HEADS = 4
R_DK = 128
R_DV = 256
N_BUCKETS = 32
RMS_EPS = 1e-6
NEG_INF = -1e30
BIG = 3.0e38
FORCE_SCORE = 1e4
Q_SCALE = HD ** -0.5
LOG2E = math.log2(math.e)
RK_SCALE = R_DK ** -0.5
EVEN_W = 4480
ODD_W = 6144
VMEM_LIMIT = 56 * 1024 * 1024
BISECT_STEPS = 16

E_QA, E_QB, E_CMP, E_SEL, E_WIN, E_KVB, E_QI, E_SM = 0, 1024, 2048, 2560, 3072, 3584, 4096, 4352
SM_KI, SM_WI, SM_GA = 0, 64, 68
O_QC, O_KVC, O_QR, O_KR, O_VR, O_GR = 0, 1024, 3072, 3584, 4096, 5120

LOG_G = [math.log1p(-(2.0 ** (-5.0 - h))) for h in range(R_HEADS)]


def _cparams(sem=None):
    return pltpu.CompilerParams(dimension_semantics=sem, vmem_limit_bytes=VMEM_LIMIT)


def _dot(a, b, precision=None):
    return jnp.dot(a, b, preferred_element_type=F32, precision=precision)


def _dot_nt(a, b, precision=None):
    return lax.dot_general(a, b, (((1,), (1,)), ((), ())), preferred_element_type=F32, precision=precision)


def _sigmoid(x):
    return 1.0 / (1.0 + jnp.exp(-x))


def _rms(x, g):
    return x * lax.rsqrt(jnp.mean(x * x, axis=-1, keepdims=True) + RMS_EPS) * g


FFN_TM, FFN_TF = 640, 512


def _ffn_kernel(x_ref, g_ref, wg_ref, wu_ref, wo_ref, o_ref, h_scr, acc_scr):
    j = pl.program_id(1)

    @pl.when(j == 0)
    def _():
        h_scr[...] = _rms(x_ref[...], g_ref[...]).astype(BF16)
        acc_scr[...] = jnp.zeros_like(acc_scr)

    h = h_scr[...]
    gate = _dot(h, wg_ref[...])
    up = _dot(h, wu_ref[...])
    act = gate * _sigmoid(gate) * up
    acc_scr[...] += _dot(act.astype(BF16), wo_ref[...])

    @pl.when(j == pl.num_programs(1) - 1)
    def _():
        o_ref[...] = x_ref[...] + 0.5 * acc_scr[...]


def _ffn_half(x, g, w_in, w_out, layer, half):
    nj = D_FF // FFN_TF
    return pl.pallas_call(
        _ffn_kernel,
        grid=(N_TOK // FFN_TM, nj),
        in_specs=[
            pl.BlockSpec((FFN_TM, D_MODEL), lambda i, j: (i, 0)),
            pl.BlockSpec((1, D_MODEL), lambda i, j: (0, 0)),
            pl.BlockSpec((None, None, D_MODEL, FFN_TF), lambda i, j: (layer, half, 0, j)),
            pl.BlockSpec((None, None, D_MODEL, FFN_TF), lambda i, j: (layer, half, 0, j + nj)),
            pl.BlockSpec((None, None, FFN_TF, D_MODEL), lambda i, j: (layer, half, j, 0)),
        ],
        out_specs=pl.BlockSpec((FFN_TM, D_MODEL), lambda i, j: (i, 0)),
        out_shape=jax.ShapeDtypeStruct((N_TOK, D_MODEL), F32),
        scratch_shapes=[pltpu.VMEM((FFN_TM, D_MODEL), BF16), pltpu.VMEM((FFN_TM, D_MODEL), F32)],
        compiler_params=_cparams(("parallel", "arbitrary")),
        name="ffn_half",
    )(x, g.reshape(1, D_MODEL), w_in, w_in, w_out)


PROJ_TM = 640


def _proj_in_kernel(x_ref, g_ref, w_ref, z_ref, zb_ref, h_scr):
    @pl.when(pl.program_id(1) == 0)
    def _():
        h_scr[...] = _rms(x_ref[...], g_ref[...]).astype(BF16)

    z = _dot(h_scr[...], w_ref[...])
    z_ref[...] = z
    zb_ref[...] = z.astype(BF16)


def _proj_in(x, g, w, tn):
    n = w.shape[1]
    return pl.pallas_call(
        _proj_in_kernel,
        grid=(N_TOK // PROJ_TM, n // tn),
        in_specs=[
            pl.BlockSpec((PROJ_TM, D_MODEL), lambda i, j: (i, 0)),
            pl.BlockSpec((1, D_MODEL), lambda i, j: (0, 0)),
            pl.BlockSpec((D_MODEL, tn), lambda i, j: (0, j)),
        ],
        out_specs=[pl.BlockSpec((PROJ_TM, tn), lambda i, j: (i, j)),
                   pl.BlockSpec((PROJ_TM, tn), lambda i, j: (i, j))],
        out_shape=[jax.ShapeDtypeStruct((N_TOK, n), F32), jax.ShapeDtypeStruct((N_TOK, n), BF16)],
        scratch_shapes=[pltpu.VMEM((PROJ_TM, D_MODEL), BF16)],
        compiler_params=_cparams(("parallel", "arbitrary")),
        name="proj_in",
    )(x, g.reshape(1, D_MODEL), w)


OUT_TM = 320


def _proj_out_kernel(x_ref, a_ref, b_ref, wa_ref, wb_ref, o_ref):
    o_ref[...] = (x_ref[...] + _dot(a_ref[...].astype(BF16), wa_ref[...])
                  + _dot(b_ref[...].astype(BF16), wb_ref[...]))


def _proj_out(x, a, b, wa, wb):
    ka, kb = a.shape[1], b.shape[1]
    return pl.pallas_call(
        _proj_out_kernel,
        grid=(N_TOK // OUT_TM,),
        in_specs=[
            pl.BlockSpec((OUT_TM, D_MODEL), lambda i: (i, 0)),
            pl.BlockSpec((OUT_TM, ka), lambda i: (i, 0)),
            pl.BlockSpec((OUT_TM, kb), lambda i: (i, 0)),
            pl.BlockSpec((ka, D_MODEL), lambda i: (0, 0)),
            pl.BlockSpec((kb, D_MODEL), lambda i: (0, 0)),
        ],
        out_specs=pl.BlockSpec((OUT_TM, D_MODEL), lambda i: (i, 0)),
        out_shape=jax.ShapeDtypeStruct((N_TOK, D_MODEL), F32),
        compiler_params=_cparams(("parallel",)),
        name="proj_out",
    )(x, a, b, wa, wb)


def _ple_kernel(x_ref, p_ref, g_ref, wg_ref, wp_ref, gf_ref, o_ref, *, final):
    x = x_ref[...]
    gate = _sigmoid(_dot(_rms(x, g_ref[...]).astype(BF16), wg_ref[...]))
    y = x + gate * _dot(p_ref[...].astype(BF16), wp_ref[...])
    o_ref[...] = _rms(y, gf_ref[...]) if final else y


def _ple(x, p, g, w_gate, w_proj, g_final, final):
    return pl.pallas_call(
        functools.partial(_ple_kernel, final=final),
        grid=(N_TOK // OUT_TM,),
        in_specs=[
            pl.BlockSpec((OUT_TM, D_MODEL), lambda i: (i, 0)),
            pl.BlockSpec((OUT_TM, PLE_DIM), lambda i: (i, 0)),
            pl.BlockSpec((1, D_MODEL), lambda i: (0, 0)),
            pl.BlockSpec((D_MODEL, D_MODEL), lambda i: (0, 0)),
            pl.BlockSpec((PLE_DIM, D_MODEL), lambda i: (0, 0)),
            pl.BlockSpec((1, D_MODEL), lambda i: (0, 0)),
        ],
        out_specs=pl.BlockSpec((OUT_TM, D_MODEL), lambda i: (i, 0)),
        out_shape=jax.ShapeDtypeStruct((N_TOK, D_MODEL), F32),
        compiler_params=_cparams(("parallel",)),
        name="ple_add",
    )(x, p, g.reshape(1, D_MODEL), w_gate, w_proj, g_final.reshape(1, D_MODEL))


def _t5_bucket_np(dist):
    dist = np.maximum(np.asarray(dist, np.int64), 0)
    exact = N_BUCKETS // 2
    out = {}
    for dt in (np.float32, np.float64):
        scaled = np.log(np.maximum(dist, exact).astype(dt) / dt(exact)) / dt(math.log(128 / exact))
        large = np.minimum(exact + (scaled * dt(N_BUCKETS - exact)).astype(np.int64), N_BUCKETS - 1)
        out[dt] = np.where(dist < exact, dist, large)
    assert np.array_equal(out[np.float32], out[np.float64])
    return out[np.float32].astype(np.int32)


def _cmp_block_of_col(col):
    return 2 * (col % 32) + col // 32


def _bias_kernel(tbl_ref, bk_near, bk_cmp, bk_dec, bk_decc, bk_dil, o_near, o_cmp, o_dec, o_decc, o_dil):
    def lookup(bucket, h):
        acc = jnp.zeros(bucket.shape, F32)
        for b in range(N_BUCKETS):
            acc = jnp.where(bucket == b, tbl_ref[b, h], acc)
        return acc

    for rel in range(3):
        bk = bk_near[rel]
        for h in range(16):
            o_near[rel, h] = lookup(bk, h) * LOG2E
    bk = bk_cmp[...]
    for h in range(8):
        o_cmp[h] = lookup(bk, h)
    bk = bk_dec[...]
    for h in range(16):
        o_dec[h:h + 1, :] = lookup(bk, h)
    bk = bk_decc[...]
    for h in range(8):
        o_decc[h:h + 1, :] = lookup(bk, h)
    lane = lax.broadcasted_iota(jnp.int32, (QB, 8), 1)
    for cfg in range(3):
        bk = bk_dil[cfg]
        acc = jnp.zeros((QB, 8), F32)
        for h in range(8):
            acc = jnp.where(lane == h, lookup(bk, h), acc)
        o_dil[cfg] = acc


DEC_W = 2 * 2176


def _bias_tables(rel_bias):
    r = np.arange(QB)[:, None]
    c = np.arange(QB)[None, :]
    bk_near = np.stack([_t5_bucket_np(rel * QB + r - c) for rel in range(3)])
    t = np.arange(SEQ)[:, None]
    col = np.arange(N_CMP)[None, :]
    bk_cmp = _t5_bucket_np(t - (CMP_BLOCK * _cmp_block_of_col(col) + CMP_BLOCK - 1))
    bk_dec = _t5_bucket_np(PAST_LEN - np.arange(DEC_W)[None, :] // 2)
    bk_decc = _t5_bucket_np(PAST_LEN - (CMP_BLOCK * _cmp_block_of_col(col) + CMP_BLOCK - 1))
    i = np.arange(QB)[:, None]
    bk_dil = np.stack([np.broadcast_to(_t5_bucket_np(dil * (QB - i)), (QB, 8)) for dil in (1, 4, 16)])
    vm = pl.BlockSpec(memory_space=pltpu.VMEM)
    return pl.pallas_call(
        _bias_kernel,
        in_specs=[pl.BlockSpec(memory_space=pltpu.SMEM), vm, vm, vm, vm, vm],
        out_specs=[vm] * 5,
        out_shape=[
            jax.ShapeDtypeStruct((3, 16, QB, QB), F32),
            jax.ShapeDtypeStruct((8, SEQ, N_CMP), F32),
            jax.ShapeDtypeStruct((16, DEC_W), F32),
            jax.ShapeDtypeStruct((8, N_CMP), F32),
            jax.ShapeDtypeStruct((3, QB, 8), F32),
        ],
        compiler_params=_cparams(),
        name="bias_tables",
    )(rel_bias, jnp.asarray(bk_near), jnp.asarray(bk_cmp), jnp.asarray(bk_dec), jnp.asarray(bk_decc),
      jnp.asarray(bk_dil))


def _compress_rows(rows_ref, pe_ref, w_ref, out_ref, nblk):
    for c in range(2):
        acc = jnp.zeros((2 * nblk, HD), F32)
        for l in range(CMP_BLOCK):
            pe = pe_ref[c, l:l + 1, :]
            xs = [rows_ref[pl.ds(4 * l + 2 * g + c, nblk, stride=4 * CMP_BLOCK), :] + pe for g in range(2)]
            acc = acc + _dot(jnp.concatenate(xs, axis=0).astype(BF16), w_ref[c, l])
        for g in range(2):
            out_ref[:, (2 * g + c) * HD:(2 * g + c + 1) * HD] = acc[g * nblk:(g + 1) * nblk]


CMP_ROWS_P = 4096


def _compress_prompt_kernel(rows_ref, pe_ref, w_ref, out_ref):
    _compress_rows(rows_ref, pe_ref, w_ref, out_ref, CMP_ROWS_P // CMP_BLOCK)


def _compress_prompt(z, pe, w):
    nblk = CMP_ROWS_P // CMP_BLOCK
    rows = z[:N_TOK_P, E_CMP:E_SEL].reshape(4 * N_TOK_P, HD)
    return pl.pallas_call(
        _compress_prompt_kernel,
        grid=(N_TOK_P // CMP_ROWS_P,),
        in_specs=[
            pl.BlockSpec((4 * CMP_ROWS_P, HD), lambda i: (i, 0)),
            pl.BlockSpec((2, CMP_BLOCK, HD), lambda i: (0, 0, 0)),
            pl.BlockSpec((2, CMP_BLOCK, HD, HD), lambda i: (0, 0, 0, 0)),
        ],
        out_specs=pl.BlockSpec((nblk, 512), lambda i: (i, 0)),
        out_shape=jax.ShapeDtypeStruct((N_TOK_P // CMP_BLOCK, 512), F32),
        compiler_params=_cparams(("parallel",)),
        name="nsa_compress_prompt",
    )(rows, pe, w)


CMP_SEQS = 4
CMP_COPIES = CMP_SEQS * N_PAGES


def _compress_sample_kernel(pt_ref, pool_ref, pe_ref, w_ref, out_ref, rows_scr, sem):
    i = pl.program_id(0)

    def page_copy(step, slot, j):
        page = pt_ref[step * CMP_SEQS + j // N_PAGES, j % N_PAGES]
        return pltpu.make_async_copy(pool_ref.at[page], rows_scr.at[slot, pl.ds(j * 4 * PAGE, 4 * PAGE)],
                                     sem.at[slot])

    def start_all(step, slot):
        for j in range(CMP_COPIES):
            page_copy(step, slot, j).start()

    @pl.when(i == 0)
    def _():
        start_all(0, 0)

    @pl.when(i + 1 < pl.num_programs(0))
    def _():
        start_all(i + 1, (i + 1) % 2)

    slot = i % 2
    for j in range(CMP_COPIES):
        page_copy(i, slot, j).wait()
    _compress_rows(rows_scr.at[slot], pe_ref, w_ref, out_ref, CMP_SEQS * N_CMP)


def _compress_sample(pool, page_table, pe, w):
    nblk = CMP_SEQS * N_CMP
    grid_spec = pltpu.PrefetchScalarGridSpec(
        num_scalar_prefetch=1,
        grid=(DEC_BATCH // CMP_SEQS,),
        in_specs=[
            pl.BlockSpec(memory_space=pl.ANY),
            pl.BlockSpec((2, CMP_BLOCK, HD), lambda i, pt: (0, 0, 0)),
            pl.BlockSpec((2, CMP_BLOCK, HD, HD), lambda i, pt: (0, 0, 0, 0)),
        ],
        out_specs=pl.BlockSpec((nblk, 512), lambda i, pt: (i, 0)),
        scratch_shapes=[pltpu.VMEM((2, 4 * CMP_SEQS * PAST_LEN, HD), F32), pltpu.SemaphoreType.DMA((2,))],
    )
    return pl.pallas_call(
        _compress_sample_kernel,
        grid_spec=grid_spec,
        out_shape=jax.ShapeDtypeStruct((DEC_BATCH * N_CMP, 512), F32),
        compiler_params=_cparams(("arbitrary",)),
        name="nsa_compress_sample",
    )(page_table, pool, pe, w)


def _even_first(cmp_sum, n_seq):
    x = cmp_sum.reshape(n_seq, N_CMP // 2, 2, 512)
    return jnp.swapaxes(x, 1, 2).reshape(n_seq, N_CMP, 512)


RG = 512
N_RG = SEQ // RG
RG_CHUNKS = RG // QB


def _attend_regions(i, streams, st, slot0=0, min_chunk=None, qk_first=False, rg_keys=RG):
    m_scr, acc_scr = st
    chunks = rg_keys // QB
    ones = jnp.ones((rg_keys, HD), BF16)
    for j in range(len(streams)):
        m_scr[slot0 + j] = jnp.full(m_scr.shape[1:], NEG_INF, F32)
        acc_scr[slot0 + j] = jnp.zeros(acc_scr.shape[1:], F32)
    for rg in range(SEQ // rg_keys):
        visit = rg * chunks <= i
        if min_chunk is not None:
            visit = visit & ((rg + 1) * chunks - 1 >= min_chunk)

        @pl.when(visit)
        def _():
            rows = slice(rg * rg_keys, (rg + 1) * rg_keys)
            qk = lambda q_bf, kv_ref, kcol: _dot_nt(q_bf, kv_ref[rows, kcol:kcol + HD])

            def pv(slot, p_bf, alpha, kv_ref, vcol):
                v1 = jnp.concatenate([kv_ref[rows, vcol:vcol + HD], ones], axis=1)
                acc_scr[slot] = alpha * acc_scr[slot] + _dot(p_bf, v1)

            raw = [qk(*stream[:3]) for stream in streams] if qk_first else None
            probs = []
            for j, (q_bf, kv_ref, kcol, vcol, logit_fn) in enumerate(streams):
                slot = slot0 + j
                s = logit_fn(rg, raw[j] if qk_first else qk(q_bf, kv_ref, kcol))
                m_old = m_scr[slot]
                m_new = jnp.maximum(m_old, jnp.max(s, axis=-1, keepdims=True))
                alpha = jnp.exp2(m_old - m_new)
                p_bf = jnp.exp2(s - m_new).astype(BF16)
                m_scr[slot] = m_new
                if qk_first:
                    probs.append((slot, p_bf, alpha, kv_ref, vcol))
                else:
                    pv(slot, p_bf, alpha, kv_ref, vcol)
            for args in probs:
                pv(*args)

    outs = []
    for j in range(len(streams)):
        acc = acc_scr[slot0 + j]
        outs.append(acc[:, :HD] / jnp.maximum(acc[:, HD:], 1e-30))
    return outs


def _softmax_state(slots, rows):
    return [pltpu.VMEM((slots, rows, 1), F32), pltpu.VMEM((slots, rows, 2 * HD), F32)]


def _mask_rows4(ok, logits):
    w = logits.shape[1]
    return jnp.where(ok[None], logits.reshape(4, QB, w), NEG_INF).reshape(4 * QB, w)


def _stack_heads(q_ref, g):
    return jnp.concatenate([q_ref[:, (4 * g + a) * HD:(4 * g + a + 1) * HD] for a in range(4)], axis=0)


def _rel_dist(i, kb):
    r = lax.broadcasted_iota(jnp.int32, (QB, QB), 0)
    c = lax.broadcasted_iota(jnp.int32, (QB, QB), 1)
    return (i - kb) * QB + r - c


def _region_dist(i, rg):
    r = lax.broadcasted_iota(jnp.int32, (QB, RG), 0)
    c = lax.broadcasted_iota(jnp.int32, (QB, RG), 1)
    return i * QB - rg * RG + r - c


def _region_bias(bnear_ref, i, rg, heads):
    tiles = [bnear_ref[jnp.clip(i - (rg * RG_CHUNKS + c), 0, 2), heads].reshape(-1, QB) for c in range(RG_CHUNKS)]
    return jnp.concatenate(tiles, axis=1)


def _split_bf16(a):
    hi = a.astype(BF16)
    return hi, (a - hi.astype(F32)).astype(BF16)


def _dot_nt3(a, b):
    ah, al = _split_bf16(a)
    bh, bl = _split_bf16(b)
    return _dot_nt(ah, bh) + (_dot_nt(ah, bl) + _dot_nt(al, bh))


def _dot3(a, b):
    ah, al = _split_bf16(a)
    bh, bl = _split_bf16(b)
    return _dot(ah, bh) + (_dot(ah, bl) + _dot(al, bh))


def _rank_select(score, n_blocks, n_take):
    blk = lax.broadcasted_iota(jnp.int32, score.shape, 1)
    rank = jnp.zeros(score.shape, F32)
    for j in range(n_blocks):
        col = score[:, j:j + 1]
        beats = (col > score) | ((col == score) & (blk > j))
        rank = rank + jnp.where(beats, 1.0, 0.0)
    return jnp.where(rank < n_take, 1.0, 0.0)


def _nsa_prompt_kernel(q_ref, sm_ref, ksel_ref, kwin_ref, cmp_ref, bnear_ref, bcmp_ref, o_ref,
                       selk_scr, m_scr, acc_scr):
    i = pl.program_id(1)
    st = (m_scr, acc_scr)
    sm = sm_ref[...]
    jj = lax.broadcasted_iota(jnp.int32, (N_CMP // 2, SEQ), 0)
    cc = lax.broadcasted_iota(jnp.int32, (N_CMP // 2, SEQ), 1)
    expand = jnp.where(jj == cc // SEL_BLOCK, 1.0, 0.0).astype(BF16)
    row = lax.broadcasted_iota(jnp.int32, (4 * QB, N_CMP), 0) & (QB - 1)
    col = lax.broadcasted_iota(jnp.int32, (4 * QB, N_CMP), 1)
    dist_c = i * QB + row - (CMP_BLOCK * _cmp_block_of_col(col) + CMP_BLOCK - 1)
    ok_c = dist_c >= 0
    t_q = i * QB + lax.broadcasted_iota(jnp.int32, (QB, N_CMP // 2), 0)
    blk = lax.broadcasted_iota(jnp.int32, (QB, N_CMP // 2), 1)
    cur = t_q // SEL_BLOCK
    forced = (blk == 0) | (blk == cur) | (blk == cur - 1)
    admissible = blk * SEL_BLOCK <= t_q
    q_bf, o_c = [], []
    for g in range(2):
        q = _stack_heads(q_ref, g) * Q_SCALE
        q_bf.append((q * LOG2E).astype(BF16))
        logit = _dot_nt3(q, cmp_ref[:, 2 * g * HD:(2 * g + 1) * HD]) + bcmp_ref[4 * g:4 * g + 4].reshape(4 * QB, N_CMP)
        logit = jnp.where(ok_c, logit, NEG_INF)
        e = jnp.where(ok_c, jnp.exp(logit - jnp.max(logit, axis=-1, keepdims=True)), 0.0)
        p_c = e / jnp.maximum(jnp.sum(e, axis=-1, keepdims=True), 1e-30)
        o_c.append(_dot(p_c.astype(BF16), cmp_ref[:, (2 * g + 1) * HD:(2 * g + 2) * HD].astype(BF16)))
        p_g = p_c[0:QB] + p_c[QB:2 * QB] + p_c[2 * QB:3 * QB] + p_c[3 * QB:4 * QB]
        score = p_g[:, :N_CMP // 2] + p_g[:, N_CMP // 2:]
        score = jnp.where(forced, FORCE_SCORE, score)
        score = jnp.where(admissible, score, NEG_INF)
        sel = _rank_select(score, SEQ // SEL_BLOCK, N_SEL).astype(BF16)
        selk_scr[g] = _dot(sel, expand)

    def sel_logits(g, rg, raw):
        ok = (selk_scr[g, :, rg * RG:(rg + 1) * RG] > 0.5) & (_region_dist(i, rg) >= 0)
        return _mask_rows4(ok, raw + _region_bias(bnear_ref, i, rg, slice(4 * g, 4 * g + 4)))

    def win_logits(g, rg, raw):
        d = _region_dist(i, rg)
        return _mask_rows4((d >= 0) & (d <= A_WINDOW), raw + _region_bias(bnear_ref, i, rg, slice(4 * g, 4 * g + 4)))

    o_s = _attend_regions(i, [(q_bf[g], ksel_ref, 2 * g * HD, (2 * g + 1) * HD, functools.partial(sel_logits, g))
                              for g in range(2)], st, slot0=0)
    o_w = _attend_regions(i, [(q_bf[g], kwin_ref, 2 * g * HD, (2 * g + 1) * HD, functools.partial(win_logits, g))
                              for g in range(2)], st, slot0=2, min_chunk=i - A_WINDOW // QB)
    for g in range(2):
        gates = []
        for br in range(3):
            gates.append(jnp.concatenate(
                [_sigmoid(sm[:, SM_GA + 3 * (4 * g + a) + br:SM_GA + 3 * (4 * g + a) + br + 1]) for a in range(4)],
                axis=0))
        out = gates[0] * o_c[g] + gates[1] * o_s[g] + gates[2] * o_w[g]
        for a in range(4):
            o_ref[:, (4 * g + a) * HD:(4 * g + a + 1) * HD] = out[a * QB:(a + 1) * QB]


def _nsa_prompt(z, zb, cmp_sum, bnear, bcmp):
    return pl.pallas_call(
        _nsa_prompt_kernel,
        grid=(BATCH, NQB),
        in_specs=[
            pl.BlockSpec((QB, 1024), lambda b, i: (b * NQB + i, E_QA // 1024)),
            pl.BlockSpec((QB, 128), lambda b, i: (b * NQB + i, E_SM // 128)),
            pl.BlockSpec((SEQ, 512), lambda b, i: (b, E_SEL // 512)),
            pl.BlockSpec((SEQ, 512), lambda b, i: (b, E_WIN // 512)),
            pl.BlockSpec((None, N_CMP, 512), lambda b, i: (b, 0, 0)),
            pl.BlockSpec((3, 8, QB, QB), lambda b, i: (0, 0, 0, 0)),
            pl.BlockSpec((8, QB, N_CMP), lambda b, i: (0, i, 0)),
        ],
        out_specs=pl.BlockSpec((QB, 1024), lambda b, i: (b * NQB + i, 0)),
        out_shape=jax.ShapeDtypeStruct((N_TOK_P, 1024), F32),
        scratch_shapes=[pltpu.VMEM((2, QB, SEQ), F32)] + _softmax_state(4, 4 * QB),
        compiler_params=_cparams(("parallel", "arbitrary")),
        name="nsa_prompt",
    )(z, z, zb, zb, cmp_sum, bnear, bcmp)


def _dsa_prompt_kernel(q_ref, qi_ref, sm_ref, kidx_ref, kv_ref, bnear_ref, o_ref,
                       sc_scr, mk_scr, m_scr, acc_scr):
    i = pl.program_id(1)
    sm = sm_ref[...]
    qi = qi_ref[...]
    wi = sm[:, SM_WI:SM_WI + 4] * 0.5

    for rg in range(N_RG):
        cols = slice(rg * RG, (rg + 1) * RG)

        @pl.when(rg * RG_CHUNKS <= i)
        def _():
            kid = kidx_ref[cols, SM_KI:SM_KI + 64]
            sc = jnp.zeros((QB, RG), F32)
            for h in range(4):
                sc = sc + wi[:, h:h + 1] * jnp.maximum(_dot_nt3(qi[:, 64 * h:64 * h + 64], kid), 0.0)
            sc_scr[:, cols] = jnp.where(_region_dist(i, rg) >= 0, sc, NEG_INF)

        @pl.when(rg * RG_CHUNKS > i)
        def _():
            sc_scr[:, cols] = jnp.full((QB, RG), NEG_INF, F32)

    scores = sc_scr[...]
    lo = jnp.min(jnp.where(scores > 0.5 * NEG_INF, scores, BIG), axis=-1, keepdims=True)
    hib = jnp.max(scores, axis=-1, keepdims=True)
    n_causal = (i * QB + 1 + lax.broadcasted_iota(jnp.int32, (QB, 1), 0)).astype(F32)
    k_eff = jnp.minimum(n_causal, float(IDX_TOPK))

    def count(thr, strict):
        s = sc_scr[...]
        return jnp.sum(jnp.where((s > thr) if strict else (s >= thr), 1.0, 0.0), axis=-1, keepdims=True)

    def max_below(bound):
        s = sc_scr[...]
        return jnp.max(jnp.where(s < bound, s, -BIG), axis=-1, keepdims=True)

    hix = jnp.full((QB, 1), BIG, F32)
    for _ in range(BISECT_STEPS):
        mid = 0.5 * (lo + hib)
        ge = count(mid, False) >= k_eff
        lo = jnp.where(ge, mid, lo)
        hib = jnp.where(ge, hib, mid)
        hix = jnp.where(ge, hix, mid)

    def step_cond(c):
        return c[3] > 0.5

    def step_body(c):
        bound, thr, done, _ = c
        cand = max_below(bound)
        ok = count(cand, False) >= k_eff
        active = done < 0.5
        thr = jnp.where(active & ok, cand, thr)
        bound = jnp.where(active & jnp.logical_not(ok), cand, bound)
        done = jnp.where(ok, 1.0, done)
        return bound, thr, done, jnp.sum(1.0 - done)

    _, thr, _, _ = lax.while_loop(step_cond, step_body,
                                  (hix, lo, jnp.zeros((QB, 1), F32), jnp.float32(QB)))
    need = k_eff - count(thr, True)
    upper = jnp.where(lax.broadcasted_iota(jnp.int32, (QB, QB), 0) <= lax.broadcasted_iota(jnp.int32, (QB, QB), 1),
                      1.0, 0.0).astype(BF16)

    carry = jnp.zeros((QB, 1), F32)
    for c in range(NQB):
        s = sc_scr[:, c * QB:(c + 1) * QB]
        eq = s == thr
        eqf = jnp.where(eq, 1.0, 0.0)
        prefix = _dot(eqf.astype(BF16), upper) + carry
        mk_scr[:, c * QB:(c + 1) * QB] = jnp.where((s > thr) | (eq & (prefix <= need)), 1.0, 0.0)
        carry = carry + jnp.sum(eqf, axis=-1, keepdims=True)

    def logits(g, rg, raw):
        bias = _region_bias(bnear_ref, i, rg, slice(4 * g, 4 * g + 4))
        return _mask_rows4(mk_scr[:, rg * RG:(rg + 1) * RG] > 0.5, raw + bias)

    streams = [((_stack_heads(q_ref, g) * Q_SCALE * LOG2E).astype(BF16), kv_ref, 2 * g * HD, (2 * g + 1) * HD,
                functools.partial(logits, g)) for g in range(2)]
    outs = _attend_regions(i, streams, (m_scr, acc_scr))
    for g in range(2):
        for a in range(4):
            o_ref[:, (4 * g + a) * HD:(4 * g + a + 1) * HD] = outs[g][a * QB:(a + 1) * QB]


def _dsa_prompt(z, zb, bnear):
    return pl.pallas_call(
        _dsa_prompt_kernel,
        grid=(BATCH, NQB),
        in_specs=[
            pl.BlockSpec((QB, 1024), lambda b, i: (b * NQB + i, E_QB // 1024)),
            pl.BlockSpec((QB, 256), lambda b, i: (b * NQB + i, E_QI // 256)),
            pl.BlockSpec((QB, 128), lambda b, i: (b * NQB + i, E_SM // 128)),
            pl.BlockSpec((SEQ, 128), lambda b, i: (b, E_SM // 128)),
            pl.BlockSpec((SEQ, 512), lambda b, i: (b, E_KVB // 512)),
            pl.BlockSpec((3, 8, QB, QB), lambda b, i: (0, 0, 0, 0)),
        ],
        out_specs=pl.BlockSpec((QB, 1024), lambda b, i: (b * NQB + i, 0)),
        out_shape=jax.ShapeDtypeStruct((N_TOK_P, 1024), F32),
        scratch_shapes=[pltpu.VMEM((QB, SEQ), F32), pltpu.VMEM((QB, SEQ), F32)] + _softmax_state(2, 4 * QB),
        compiler_params=_cparams(("parallel", "arbitrary")),
        name="dsa_prompt",
    )(z, z, z, z, zb, bnear)


DIL_CFG = ((128, 1), (512, 4), (2048, 16))
DIL_REL = 5
DIL_RG = 1024


def _dil_count(dist):
    cnt = jnp.zeros(dist.shape, F32)
    for window, dil in DIL_CFG:
        cnt = cnt + jnp.where((dist >= 0) & (dist <= window) & ((dist & (dil - 1)) == 0), 1.0, 0.0)
    return cnt


def _dil_prompt_kernel(q_ref, kva_ref, kvb_ref, bnear_ref, o_ref, tab_scr, m_scr, acc_scr):
    i = pl.program_id(1)

    @pl.when(i == 0)
    def _():
        for rel in range(DIL_REL + 2):
            if rel <= DIL_REL:
                cnt = _dil_count(_rel_dist(rel, 0))
                log_n = jnp.where(cnt > 2.5, math.log2(3.0), jnp.where(cnt > 1.5, 1.0,
                                                                       jnp.where(cnt > 0.5, 0.0, NEG_INF)))
            else:
                log_n = jnp.full((QB, QB), NEG_INF, F32)
            for h in range(8):
                tab_scr[rel, h] = bnear_ref[min(rel, 2), h] + log_n

    def logits(h, rg, raw):
        tiles = []
        for c in range(DIL_RG // QB):
            rel = i - (rg * (DIL_RG // QB) + c)
            tiles.append(tab_scr[jnp.where(rel < 0, DIL_REL + 1, jnp.minimum(rel, DIL_REL)), h])
        return raw + jnp.concatenate(tiles, axis=1)

    streams = []
    for h in range(8):
        col = (h % 4) * 2 * HD
        q_bf = (q_ref[:, h * HD:(h + 1) * HD] * Q_SCALE * LOG2E).astype(BF16)
        streams.append((q_bf, kva_ref if h < 4 else kvb_ref, col, col + HD, functools.partial(logits, h)))
    outs = _attend_regions(i, streams, (m_scr, acc_scr), qk_first=True, rg_keys=DIL_RG)
    for h in range(8):
        o_ref[:, h * HD:(h + 1) * HD] = outs[h]


def _dil_prompt(z, zb, bnear):
    return pl.pallas_call(
        _dil_prompt_kernel,
        grid=(BATCH, NQB),
        in_specs=[
            pl.BlockSpec((QB, 1024), lambda b, i: (b * NQB + i, O_QC // 1024)),
            pl.BlockSpec((SEQ, 1024), lambda b, i: (b, O_KVC // 1024)),
            pl.BlockSpec((SEQ, 1024), lambda b, i: (b, O_KVC // 1024 + 1)),
            pl.BlockSpec((3, 8, QB, QB), lambda b, i: (0, 0, 0, 0)),
        ],
        out_specs=pl.BlockSpec((QB, 1024), lambda b, i: (b * NQB + i, 0)),
        out_shape=jax.ShapeDtypeStruct((N_TOK_P, 1024), F32),
        scratch_shapes=[pltpu.VMEM((DIL_REL + 2, 8, QB, QB), F32)] + _softmax_state(8, QB),
        compiler_params=_cparams(("arbitrary", "arbitrary")),
        name="dilated_prompt",
    )(z, zb, zb, bnear)


def _rope_rows(x, cos2, sin2):
    return x * cos2 + pltpu.roll(x, 64, 1) * sin2


def _readout(o, gate, gn):
    mu = jnp.mean(o, axis=-1, keepdims=True)
    d = o - mu
    var = jnp.mean(d * d, axis=-1, keepdims=True)
    return gate * _sigmoid(gate) * (d * lax.rsqrt(var + RMS_EPS)) * gn


def _ret_prompt_kernel(q_ref, k_ref, v_ref, gr_ref, cos_ref, sin_ref, gn_ref, o_ref, st_ref, st_scr):
    ci = pl.program_id(1)

    @pl.when(ci == 0)
    def _():
        st_scr[...] = jnp.zeros_like(st_scr)

    cos2, sin2 = cos_ref[...], sin_ref[...]
    r = lax.broadcasted_iota(jnp.int32, (QB, QB), 0)
    c = lax.broadcasted_iota(jnp.int32, (QB, QB), 1)
    rel = (r - c).astype(F32)
    pos_col = (lax.broadcasted_iota(jnp.int32, (QB, 1), 0)).astype(F32)
    for h in range(R_HEADS):
        lg = LOG_G[h]
        q = _rope_rows(q_ref[:, h * R_DK:(h + 1) * R_DK], cos2, sin2)
        k = _rope_rows(k_ref[:, h * R_DK:(h + 1) * R_DK], cos2, sin2) * RK_SCALE
        v = v_ref[:, h * R_DV:(h + 1) * R_DV]
        state = st_scr[h]
        decay = jnp.where(r >= c, jnp.exp(lg * jnp.maximum(rel, 0.0)), 0.0)
        q_bf, v_bf = q.astype(BF16), v.astype(BF16)
        scores = _dot_nt(q_bf, k.astype(BF16)) * decay
        out = _dot(scores.astype(BF16), v_bf)
        out = out + _dot(q_bf, state.astype(BF16)) * jnp.exp(lg * (pos_col + 1.0))
        zeta = jnp.exp(lg * (QB - 1.0 - pos_col))
        new_state = state * math.exp(lg * QB) + _dot((k * zeta).T.astype(BF16), v_bf)
        st_scr[h] = new_state
        st_ref[h] = new_state
        o_ref[:, h * R_DV:(h + 1) * R_DV] = _readout(out, gr_ref[:, h * R_DV:(h + 1) * R_DV],
                                                     gn_ref[:, h * R_DV:(h + 1) * R_DV])


def _ret_prompt(z, cos2, sin2, gn):
    return pl.pallas_call(
        _ret_prompt_kernel,
        grid=(BATCH, NQB),
        in_specs=[
            pl.BlockSpec((QB, 512), lambda b, i: (b * NQB + i, O_QR // 512)),
            pl.BlockSpec((QB, 512), lambda b, i: (b * NQB + i, O_KR // 512)),
            pl.BlockSpec((QB, 1024), lambda b, i: (b * NQB + i, O_VR // 1024)),
            pl.BlockSpec((QB, 1024), lambda b, i: (b * NQB + i, O_GR // 1024)),
            pl.BlockSpec((QB, 128), lambda b, i: (i, 0)),
            pl.BlockSpec((QB, 128), lambda b, i: (i, 0)),
            pl.BlockSpec((1, 1024), lambda b, i: (0, 0)),
        ],
        out_specs=[pl.BlockSpec((QB, 1024), lambda b, i: (b * NQB + i, 0)),
                   pl.BlockSpec((None, R_HEADS, R_DK, R_DV), lambda b, i: (b, 0, 0, 0))],
        out_shape=[jax.ShapeDtypeStruct((N_TOK_P, 1024), F32),
                   jax.ShapeDtypeStruct((BATCH, R_HEADS, R_DK, R_DV), F32)],
        scratch_shapes=[pltpu.VMEM((R_HEADS, R_DK, R_DV), F32)],
        compiler_params=_cparams(("parallel", "arbitrary")),
        name="retention_prompt",
    )(z, z, z, z, cos2, sin2, gn.reshape(1, 1024))


DI_QA, DI_QB, DI_KSEL, DI_VSEL, DI_KWIN, DI_VWIN, DI_KB, DI_VB, DI_QI, DI_KI, DI_MISC = (
    0, 128, 256, 384, 512, 640, 768, 896, 1024, 1088, 1152)
DEC_IN_W = 1280
DEC_SEL_BLOCKS = PAST_LEN // SEL_BLOCK + 1
DEC_SPLIT, DEC_ROUNDS = 16, 4


def _kv_rows(ref, start, kv):
    return ref[pl.ds(4 * start + kv, 2 * PAGE, stride=2), :].astype(BF16)


def _decode_attend(q, q_bf, chunks, k_new, v_new, bias_new, new_ok, logit_fn):
    row = lax.broadcasted_iota(jnp.int32, (8, 2 * PAGE), 0)
    col = lax.broadcasted_iota(jnp.int32, (8, 2 * PAGE), 1)
    own = (col & 1) == jnp.where(row < 4, 0, 1)
    logits = []
    for k, (ref, start) in enumerate(chunks):
        logits.append(jnp.where(own, logit_fn(k, _dot_nt(q_bf, _kv_rows(ref, start, 0))), NEG_INF))
    s_new = jnp.sum(q * k_new, axis=-1, keepdims=True) + bias_new
    if new_ok is not None:
        s_new = jnp.where(new_ok, s_new, NEG_INF)
    m = s_new
    for s in logits:
        m = jnp.maximum(m, jnp.max(s, axis=-1, keepdims=True))
    p_new = jnp.exp(s_new - m)
    l = p_new
    acc = p_new * v_new
    for s, (ref, start) in zip(logits, chunks):
        p = jnp.exp(s - m)
        l = l + jnp.sum(p, axis=-1, keepdims=True)
        acc = acc + _dot(p.astype(BF16), _kv_rows(ref, start, 1))
    return acc / jnp.maximum(l, 1e-30)


def _even_sample_kernel(pt_ref, *refs):
    sel_pages = refs[0:N_PAGES]
    kvb_pages = refs[N_PAGES:2 * N_PAGES]
    idx_pages = refs[2 * N_PAGES:3 * N_PAGES]
    din_ref, cmp_ref, win_ref, expand_ref, dec_ref, decc_ref, o_ref = refs[3 * N_PAGES:]
    d = din_ref[...]
    row = lax.broadcasted_iota(jnp.int32, (8, QB), 0)
    row_c = lax.broadcasted_iota(jnp.int32, (8, N_CMP), 0)
    pick_c = lambda a, b: jnp.where(row_c < 4, a, b)
    pick = lambda a, b: jnp.where(row < 4, a, b)
    dec_a, dec_b = dec_ref[0:8, :], dec_ref[8:16, :]

    qa = d[:, DI_QA:DI_QA + HD] * Q_SCALE
    qa_bf = qa.astype(BF16)
    cmp = cmp_ref[...]
    logit = pick_c(_dot_nt3(qa, cmp[:, 0:HD]), _dot_nt3(qa, cmp[:, 2 * HD:3 * HD])) + decc_ref[...]
    e = jnp.exp(logit - jnp.max(logit, axis=-1, keepdims=True))
    p_c = e / jnp.maximum(jnp.sum(e, axis=-1, keepdims=True), 1e-30)
    p_bf = p_c.astype(BF16)
    cmp_bf = cmp.astype(BF16)
    o_c = pick(_dot(p_bf, cmp_bf[:, HD:2 * HD]), _dot(p_bf, cmp_bf[:, 3 * HD:4 * HD]))
    g0 = jnp.sum(jnp.where(row_c < 4, p_c, 0.0), axis=0, keepdims=True)
    g1 = jnp.sum(jnp.where(row_c < 4, 0.0, p_c), axis=0, keepdims=True)
    p_g = pick_c(jnp.broadcast_to(g0, (8, N_CMP)), jnp.broadcast_to(g1, (8, N_CMP)))
    blk = lax.broadcasted_iota(jnp.int32, (8, N_CMP), 1)
    score = jnp.concatenate([p_g[:, :N_CMP // 2] + p_g[:, N_CMP // 2:], jnp.zeros((8, N_CMP // 2), F32)], axis=1)
    cur = PAST_LEN // SEL_BLOCK
    score = jnp.where((blk == 0) | (blk == cur) | (blk == cur - 1), FORCE_SCORE, score)
    score = jnp.where(blk < DEC_SEL_BLOCKS, score, -BIG)
    sel = _rank_select(score, DEC_SEL_BLOCKS, N_SEL) * jnp.where(blk < DEC_SEL_BLOCKS, 1.0, 0.0)
    sel_pos = _dot(sel.astype(BF16), expand_ref[...])
    cols = lambda k: slice(2 * k * PAGE, 2 * (k + 1) * PAGE)
    new_col = slice(2 * PAST_LEN, 2 * PAST_LEN + 1)

    def sel_logits(k, raw):
        return jnp.where(sel_pos[:, cols(k)] > 0.5, raw + dec_a[:, cols(k)], NEG_INF)

    o_s = _decode_attend(qa, qa_bf, [(pg, 0) for pg in sel_pages], d[:, DI_KSEL:DI_KSEL + HD],
                         d[:, DI_VSEL:DI_VSEL + HD], dec_a[:, new_col], None, sel_logits)
    win0 = (PAST_LEN - A_WINDOW) // PAGE

    def win_logits(k, raw):
        return raw + dec_a[:, cols(win0 + k)]

    o_w = _decode_attend(qa, qa_bf, [(win_ref, k * PAGE) for k in range(A_WINDOW // PAGE)],
                         d[:, DI_KWIN:DI_KWIN + HD], d[:, DI_VWIN:DI_VWIN + HD], dec_a[:, new_col], None, win_logits)
    gate = _sigmoid(d[:, DI_MISC:DI_MISC + 3])
    o_ref[:, 0:HD] = gate[:, 0:1] * o_c + gate[:, 1:2] * o_s + gate[:, 2:3] * o_w

    qi = d[:, DI_QI:DI_QI + 64]
    wi = d[:, DI_MISC + 3:DI_MISC + 4] * 0.5
    row16 = lax.broadcasted_iota(jnp.int32, (N_PAGES, PAGE), 0)
    sc = jnp.zeros((N_PAGES, PAGE), F32)
    for k, pg in enumerate(idx_pages):
        sc_k = jnp.sum(wi * jnp.maximum(_dot3(qi, pg[...]), 0.0), axis=0, keepdims=True)
        sc = jnp.where(row16 == k, sc_k, sc)
    sc_new = jnp.sum(wi * jnp.maximum(jnp.sum(qi * d[:, DI_KI:DI_KI + 64], axis=-1, keepdims=True), 0.0),
                     axis=0, keepdims=True)
    total = lambda x: jnp.sum(jnp.sum(x, axis=-1, keepdims=True), axis=0, keepdims=True)

    def count(thr, strict):
        hit = (lambda x: x > thr) if strict else (lambda x: x >= thr)
        return total(jnp.where(hit(sc), 1.0, 0.0)) + jnp.where(hit(sc_new), 1.0, 0.0)

    def max_below(bound):
        best = jnp.max(jnp.max(jnp.where(sc < bound, sc, -BIG), axis=-1, keepdims=True), axis=0, keepdims=True)
        return jnp.maximum(best, jnp.where(sc_new < bound, sc_new, -BIG))

    k_top = float(IDX_TOPK)
    lo = jnp.minimum(jnp.min(jnp.min(sc, axis=-1, keepdims=True), axis=0, keepdims=True), sc_new)
    hib = jnp.maximum(jnp.max(jnp.max(sc, axis=-1, keepdims=True), axis=0, keepdims=True), sc_new)
    hix = jnp.full((1, 1), BIG, F32)
    frac = (lax.broadcasted_iota(jnp.int32, (DEC_SPLIT, 1, 1), 0) + 1).astype(F32) * (1.0 / DEC_SPLIT)
    for _ in range(DEC_ROUNDS):
        tk = lo + (hib - lo) * frac
        cnt = jnp.sum(jnp.sum(jnp.where(sc[None] >= tk, 1.0, 0.0), axis=2, keepdims=True), axis=1, keepdims=True)
        ge = cnt + jnp.where(sc_new >= tk, 1.0, 0.0) >= k_top
        fail = jnp.min(jnp.where(ge, BIG, tk), axis=0)
        lo = jnp.max(jnp.where(ge, tk, lo), axis=0)
        hib = jnp.minimum(hib, fail)
        hix = jnp.minimum(hix, fail)

    def step_body(c):
        bound, thr, _ = c
        cand = max_below(bound)
        ok = count(cand, False) >= k_top
        return jnp.where(ok, bound, cand), jnp.where(ok, cand, thr), jnp.sum(jnp.where(ok, 0.0, 1.0))

    _, thr, _ = lax.while_loop(lambda c: c[2] > 0.5, step_body, (hix, lo, jnp.float32(1.0)))
    n_gt = count(thr, True)
    need = k_top - n_gt
    eq = jnp.where(sc == thr, 1.0, 0.0).astype(BF16)
    r128 = lax.broadcasted_iota(jnp.int32, (QB, QB), 0)
    c128 = lax.broadcasted_iota(jnp.int32, (QB, QB), 1)
    upper = jnp.where(r128 <= c128, 1.0, 0.0).astype(BF16)
    r16 = lax.broadcasted_iota(jnp.int32, (N_PAGES, N_PAGES), 0)
    c16 = lax.broadcasted_iota(jnp.int32, (N_PAGES, N_PAGES), 1)
    lower = jnp.where(c16 < r16, 1.0, 0.0).astype(BF16)
    prefix = _dot(eq, upper) + jnp.sum(_dot(lower, eq), axis=-1, keepdims=True)
    keep = jnp.where((sc > thr) | ((sc == thr) & (prefix <= need)), 1.0, 0.0)
    n_eq_past = total(eq.astype(F32))
    new_ok = (sc_new > thr) | ((sc_new == thr) & (n_gt + n_eq_past + 1.0 <= k_top))
    pos = lax.broadcasted_iota(jnp.int32, (PAGE, 2 * PAGE), 0)
    col2 = lax.broadcasted_iota(jnp.int32, (PAGE, 2 * PAGE), 1)
    keep2 = _dot(keep.astype(BF16), jnp.where(col2 // 2 == pos, 1.0, 0.0).astype(BF16))

    qb = d[:, DI_QB:DI_QB + HD] * Q_SCALE

    def b_logits(k, raw):
        return jnp.where(keep2[k:k + 1, :] > 0.5, raw + dec_b[:, cols(k)], NEG_INF)

    o_ref[:, HD:2 * HD] = _decode_attend(
        qb, qb.astype(BF16), [(pg, 0) for pg in kvb_pages], d[:, DI_KB:DI_KB + HD], d[:, DI_VB:DI_VB + HD],
        dec_b[:, new_col], new_ok, b_logits)


def _seq_page_map(k, s, pt_ref):
    return (pt_ref[s, k], 0, 0)


def _even_sample(page_table, pool_sel, pool_b, pool_idx, din, cmp_sum, win_buf, expand, dec, decc):
    specs = []
    for shape in ((4 * PAGE, HD), (4 * PAGE, HD), (64, PAGE)):
        specs += [pl.BlockSpec((None,) + shape, functools.partial(_seq_page_map, k)) for k in range(N_PAGES)]
    specs += [
        pl.BlockSpec((None, 8, DEC_IN_W), lambda s, pt: (s, 0, 0)),
        pl.BlockSpec((None, N_CMP, 512), lambda s, pt: (s, 0, 0)),
        pl.BlockSpec((None, 4 * A_WINDOW, HD), lambda s, pt: (s, 0, 0)),
        pl.BlockSpec((N_CMP, 2 * PAST_LEN), lambda s, pt: (0, 0)),
        pl.BlockSpec((16, DEC_W), lambda s, pt: (0, 0)),
        pl.BlockSpec((8, N_CMP), lambda s, pt: (0, 0)),
    ]
    grid_spec = pltpu.PrefetchScalarGridSpec(
        num_scalar_prefetch=1, grid=(DEC_BATCH,), in_specs=specs,
        out_specs=pl.BlockSpec((None, 8, 2 * HD), lambda s, pt: (s, 0, 0)))
    return pl.pallas_call(
        _even_sample_kernel,
        grid_spec=grid_spec,
        out_shape=jax.ShapeDtypeStruct((DEC_BATCH, 8, 2 * HD), F32),
        compiler_params=_cparams(("parallel",)),
        name="even_sample",
    )(page_table, *([pool_sel] * N_PAGES), *([pool_b] * N_PAGES), *([pool_idx] * N_PAGES),
      din, cmp_sum, win_buf, expand, dec, decc)


def _odd_sample_kernel(c0_ref, c1_ref, c2_ref, qkv_ref, rrow_ref, kcol_ref, st_ref, bdil_ref, tbl_ref,
                       rope_ref, ropec_ref, gn_ref, oc_ref, od_ref, st_out_ref):
    qkv = qkv_ref[...]
    q = qkv[:, 0:HD] * Q_SCALE
    lane = lax.broadcasted_iota(jnp.int32, (QB, 8), 1)
    lane1 = lax.broadcasted_iota(jnp.int32, (1, 8), 1)
    cfg_refs = (c0_ref, c1_ref, c2_ref)
    head_rows = lambda cfg, h, kv: cfg_refs[cfg][:, 2 * h + kv, :]
    logits = []
    for cfg in range(3):
        s = bdil_ref[cfg]
        for h in range(8):
            s_h = jnp.sum(head_rows(cfg, h, 0) * q[h:h + 1, :], axis=-1, keepdims=True)
            s = s + jnp.where(lane == h, s_h, 0.0)
        logits.append(s)
    s_new = tbl_ref[0:1, 0:8]
    for h in range(8):
        s_h = jnp.sum(q[h:h + 1, :] * qkv[h:h + 1, HD:2 * HD], axis=-1, keepdims=True)
        s_new = s_new + jnp.where(lane1 == h, s_h, 0.0)
    m = s_new
    for s in logits:
        m = jnp.maximum(m, jnp.max(s, axis=0, keepdims=True))
    p_new = 3.0 * jnp.exp(s_new - m)
    probs = [jnp.exp(s - m) for s in logits]
    l = p_new
    for p in probs:
        l = l + jnp.sum(p, axis=0, keepdims=True)
    inv = 1.0 / jnp.maximum(l, 1e-30)
    for h in range(8):
        acc = p_new[:, h:h + 1] * qkv[h:h + 1, 2 * HD:3 * HD]
        for cfg in range(3):
            v = head_rows(cfg, h, 1)
            acc = acc + jnp.sum(probs[cfg][:, h:h + 1] * v, axis=0, keepdims=True)
        oc_ref[:, h * HD:(h + 1) * HD] = acc * inv[:, h:h + 1]

    rrow = rrow_ref[...]
    cos2, sin2 = rope_ref[0:1, :], rope_ref[1:2, :]
    qr = _rope_rows(rrow[:, 0:R_DK], cos2, sin2)
    kr = _rope_rows(rrow[:, R_DK:2 * R_DK], cos2, sin2) * RK_SCALE
    qk = jnp.sum(qr * kr, axis=-1, keepdims=True)
    kc = kcol_ref[...]
    kc = (kc * ropec_ref[:, 0:1] + jnp.concatenate([kc[64:], kc[:64]], axis=0) * ropec_ref[:, 1:2]) * RK_SCALE
    qr_bf = jnp.concatenate([qr, jnp.zeros_like(qr)], axis=0).astype(BF16)
    row8 = lax.broadcasted_iota(jnp.int32, (8, R_DV), 0)
    for h in range(R_HEADS):
        gamma = math.exp(LOG_G[h])
        state = st_ref[h]
        v = rrow[h:h + 1, 2 * R_DK:2 * R_DK + R_DV]
        q_state = jnp.sum(jnp.where(row8 == h, _dot(qr_bf, state.astype(BF16)), 0.0), axis=0, keepdims=True)
        out = qk[h:h + 1, :] * v + q_state * gamma
        st_out_ref[h] = state * gamma + kc[:, h:h + 1] * v
        od_ref[:, h * R_DV:(h + 1) * R_DV] = _readout(out, rrow[h:h + 1, 2 * R_DK + R_DV:2 * R_DK + 2 * R_DV],
                                                      gn_ref[:, h * R_DV:(h + 1) * R_DV])


def _odd_sample(c_buf, qkv, rrow, kcol, state, bdil, rel_bias, rope_row, rope_col, gn):
    views = [c_buf.reshape(DEC_BATCH, PAST_LEN // dil, dil * 16, HD) for _, dil in DIL_CFG]
    c_specs = [pl.BlockSpec((None, QB, 16, HD), functools.partial(lambda nb, s: (s, nb, 0, 0), PAST_LEN // dil // QB - 1))
               for _, dil in DIL_CFG]
    return pl.pallas_call(
        _odd_sample_kernel,
        grid=(DEC_BATCH,),
        in_specs=c_specs + [
            pl.BlockSpec((None, 8, 3 * HD), lambda s: (s, 0, 0)),
            pl.BlockSpec((None, R_HEADS, 2 * R_DK + 2 * R_DV), lambda s: (s, 0, 0)),
            pl.BlockSpec((None, R_DK, R_HEADS), lambda s: (s, 0, 0)),
            pl.BlockSpec((None, R_HEADS, R_DK, R_DV), lambda s: (s, 0, 0, 0)),
            pl.BlockSpec((3, QB, 8), lambda s: (0, 0, 0)),
            pl.BlockSpec((N_BUCKETS, 16), lambda s: (0, 0)),
            pl.BlockSpec((2, R_DK), lambda s: (0, 0)),
            pl.BlockSpec((R_DK, 2), lambda s: (0, 0)),
            pl.BlockSpec((1, 1024), lambda s: (0, 0)),
        ],
        out_specs=[pl.BlockSpec((None, 1, 1024), lambda s: (s, 0, 0)),
                   pl.BlockSpec((None, 1, 1024), lambda s: (s, 0, 0)),
                   pl.BlockSpec((None, R_HEADS, R_DK, R_DV), lambda s: (s, 0, 0, 0))],
        out_shape=[jax.ShapeDtypeStruct((DEC_BATCH, 1, 1024), F32),
                   jax.ShapeDtypeStruct((DEC_BATCH, 1, 1024), F32),
                   jax.ShapeDtypeStruct((DEC_BATCH, R_HEADS, R_DK, R_DV), F32)],
        compiler_params=_cparams(("parallel",)),
        name="odd_sample",
    )(*views, qkv, rrow, kcol, state, bdil, rel_bias, rope_row, rope_col, gn.reshape(1, 1024))


def _rope_tables(pos):
    half = R_DK // 2
    inv = 10000.0 ** (-jnp.arange(half, dtype=F32) / half)
    ang = pos.astype(F32)[:, None] * inv[None, :]
    cos, sin = jnp.cos(ang), jnp.sin(ang)
    return jnp.concatenate([cos, cos], axis=-1), jnp.concatenate([-sin, sin], axis=-1)


def _permute_even_w(w):
    qa, kva, ga, qb, kvb, qi, ki, wi = jnp.split(w, [1024, 2560, 2584, 3608, 4120, 4376, 4440], axis=1)
    pad = jnp.zeros((w.shape[0], EVEN_W - w.shape[1]), w.dtype)
    return jnp.concatenate([qa, qb, kva, kvb, qi, ki, wi, ga, pad], axis=1)


def _even_decode(zs, e, bias, page_table, cache_a_cmp, cache_a_sel, cache_b_kv, cache_b_idx, state_a_win,
                 cmp_pe, cmp_w_bf):
    dec, decc = bias[2], bias[3]
    n_pool = cache_a_cmp.shape[1]
    cmp_s = _even_first(_compress_sample(cache_a_cmp[e].reshape(n_pool, 4 * PAGE, HD), page_table, cmp_pe, cmp_w_bf),
                        DEC_BATCH)
    heads = lambda a: a.reshape(DEC_BATCH, 8, HD)
    rep = lambda a: jnp.repeat(a.reshape(DEC_BATCH, 2, HD), 4, axis=1)
    kva = zs[:, E_CMP:E_KVB].reshape(DEC_BATCH, 3, 2, 2, HD)
    kvb = zs[:, E_KVB:E_QI].reshape(DEC_BATCH, 2, 2, HD)
    sm = zs[:, E_SM:E_SM + 128]
    zero4 = lambda a: jnp.concatenate([a, jnp.zeros_like(a)], axis=1)
    qi8 = zero4(zs[:, E_QI:E_SM].reshape(DEC_BATCH, 4, 64))
    ki8 = jnp.broadcast_to(sm[:, None, SM_KI:SM_KI + 64], (DEC_BATCH, 8, 64))
    gates = sm[:, SM_GA:SM_GA + 24].reshape(DEC_BATCH, 8, 3)
    wi8 = zero4(sm[:, SM_WI:SM_WI + 4].reshape(DEC_BATCH, 4, 1))
    misc = jnp.concatenate([gates, wi8, jnp.zeros((DEC_BATCH, 8, DEC_IN_W - DI_MISC - 4), F32)], axis=2)
    din = jnp.concatenate([
        heads(zs[:, E_QA:E_QB]), heads(zs[:, E_QB:E_CMP]),
        rep(kva[:, 1, :, 0]), rep(kva[:, 1, :, 1]), rep(kva[:, 2, :, 0]), rep(kva[:, 2, :, 1]),
        rep(kvb[:, :, 0]), rep(kvb[:, :, 1]), qi8, ki8, misc], axis=2)
    expand = (jnp.arange(2 * PAST_LEN)[None, :] // (2 * SEL_BLOCK) == jnp.arange(N_CMP)[:, None]).astype(BF16)
    o_s = _even_sample(page_table, cache_a_sel[e].reshape(n_pool, 4 * PAGE, HD),
                       cache_b_kv[e].reshape(n_pool, 4 * PAGE, HD), jnp.swapaxes(cache_b_idx[e], 1, 2),
                       din, cmp_s, state_a_win[e].reshape(DEC_BATCH, 4 * A_WINDOW, HD), expand, dec, decc)
    return o_s[:, :, :HD].reshape(DEC_BATCH, 1024), o_s[:, :, HD:].reshape(DEC_BATCH, 1024)


def _even_layer(x, e, bias, page_table, cache_a_cmp, cache_a_sel, cache_b_kv, cache_b_idx, state_a_win,
                g, w_in, w_out, cmp_pe, cmp_w):
    bnear, bcmp = bias[0], bias[1]
    z, zb = _proj_in(x, g, _permute_even_w(w_in).astype(BF16), 896)
    cmp_w_bf = cmp_w.astype(BF16)
    cmp_p = _even_first(_compress_prompt(z, cmp_pe, cmp_w_bf), BATCH)
    oa_p = _nsa_prompt(z, zb, cmp_p, bnear[:, 0:8], bcmp)
    ob_p = _dsa_prompt(z, zb, bnear[:, 8:16])
    zs = z[N_TOK_P:]
    oa_s, ob_s = _even_decode(zs, e, bias, page_table, cache_a_cmp, cache_a_sel, cache_b_kv, cache_b_idx,
                              state_a_win, cmp_pe, cmp_w_bf)
    oa = jnp.concatenate([oa_p, oa_s], axis=0)
    ob = jnp.concatenate([ob_p, ob_s], axis=0)
    x = _proj_out(x, oa, ob, w_out[:1024].astype(BF16), w_out[1024:].astype(BF16))

    kv5 = lambda a, n: a.reshape(n, -1, 2, 2, HD)
    zp = lambda lo, hi: z[:N_TOK_P, lo:hi]
    state = (
        kv5(zp(E_CMP, E_SEL), BATCH), kv5(zs[:, E_CMP:E_SEL], DEC_BATCH),
        kv5(zp(E_SEL, E_WIN), BATCH), kv5(zs[:, E_SEL:E_WIN], DEC_BATCH),
        kv5(zp(E_KVB, E_QI), BATCH), kv5(zs[:, E_KVB:E_QI], DEC_BATCH),
        zp(E_SM, E_SM + 64).reshape(BATCH, SEQ, 64), zs[:, None, E_SM:E_SM + 64],
        kv5(zp(E_WIN, E_KVB).reshape(BATCH, SEQ, 512)[:, SEQ - A_WINDOW:], BATCH), kv5(zs[:, E_WIN:E_KVB], DEC_BATCH),
    )
    return x, state


def _odd_layer(x, o, bias, state_c_win, state_ret, g, w_in, w_out, gn):
    bnear, bdil = bias[0], bias[4]
    z, zb = _proj_in(x, g, w_in.astype(BF16), 1536)
    cos_p, sin_p = _rope_tables(jnp.arange(SEQ))
    oc_p = _dil_prompt(z, zb, bnear[:, 0:8])
    od_p, ret_p = _ret_prompt(z, cos_p, sin_p, gn)

    zs = z[N_TOK_P:]
    kvc = zs[:, O_KVC:O_QR].reshape(DEC_BATCH, 8, 2, HD)
    qkv = jnp.concatenate([zs[:, O_QC:O_KVC].reshape(DEC_BATCH, 8, HD), kvc[:, :, 0], kvc[:, :, 1]], axis=2)
    rrow = jnp.concatenate([zs[:, O_QR:O_KR].reshape(DEC_BATCH, R_HEADS, R_DK),
                            zs[:, O_KR:O_VR].reshape(DEC_BATCH, R_HEADS, R_DK),
                            zs[:, O_VR:O_GR].reshape(DEC_BATCH, R_HEADS, R_DV),
                            zs[:, O_GR:].reshape(DEC_BATCH, R_HEADS, R_DV)], axis=2)
    kcol = jnp.swapaxes(zs[:, O_KR:O_VR].reshape(DEC_BATCH, R_HEADS, R_DK), 1, 2)
    cos_s, sin_s = _rope_tables(jnp.full((1,), PAST_LEN))
    rope_row = jnp.concatenate([cos_s, sin_s], axis=0)
    oc_s, od_s, ret_s = _odd_sample(state_c_win[o], qkv, rrow, kcol,
                                    state_ret[o], bdil, bias[5], rope_row, rope_row.T, gn)
    oc = jnp.concatenate([oc_p, oc_s.reshape(DEC_BATCH, 1024)], axis=0)
    od = jnp.concatenate([od_p, od_s.reshape(DEC_BATCH, 1024)], axis=0)
    x = _proj_out(x, oc, od, w_out[:1024].astype(BF16), w_out[1024:].astype(BF16))
    state = (z[:N_TOK_P, O_KVC:O_QR].reshape(BATCH, SEQ, 8, 2, HD), kvc[:, None], ret_p, ret_s)
    return x, state


def kernel(x_prompt, x_sample, cache_a_cmp, cache_a_sel, cache_b_kv, cache_b_idx, state_a_win, state_c_win,
           state_ret, page_table, p_prompt, p_sample, rel_bias, norm_g, final_norm, ffn_w_in, ffn_w_out,
           ple_gate, ple_proj, even_w_in, even_w_out, nsa_cmp_pe, nsa_cmp_w, odd_w_in, odd_w_out, ret_gn):
    depth = norm_g.shape[0]
    x = jnp.concatenate([x_prompt.reshape(N_TOK_P, D_MODEL), x_sample.reshape(DEC_BATCH, D_MODEL)], axis=0)
    bias = tuple(_bias_tables(rel_bias)) + (rel_bias,)
    w_in_bf = ffn_w_in.astype(BF16)
    w_out_bf = ffn_w_out.astype(BF16)
    even_states, odd_states = [], []
    for i in range(depth):
        x = _ffn_half(x, norm_g[i, 0], w_in_bf, w_out_bf, i, 0)
        if i % 2 == 0:
            e = i // 2
            x, st = _even_layer(x, e, bias, page_table, cache_a_cmp, cache_a_sel, cache_b_kv, cache_b_idx,
                                state_a_win, norm_g[i, 1], even_w_in[e], even_w_out[e], nsa_cmp_pe[e], nsa_cmp_w[e])
            even_states.append(st)
        else:
            o = i // 2
            x, st = _odd_layer(x, o, bias, state_c_win, state_ret, norm_g[i, 1], odd_w_in[o], odd_w_out[o],
                               ret_gn[o])
            odd_states.append(st)
        x = _ffn_half(x, norm_g[i, 2], w_in_bf, w_out_bf, i, 1)
        p = jnp.concatenate([p_prompt[i].reshape(N_TOK_P, PLE_DIM), p_sample[i].reshape(DEC_BATCH, PLE_DIM)], axis=0)
        x = _ple(x, p, norm_g[i, 3], ple_gate[i].astype(BF16), ple_proj[i].astype(BF16), final_norm,
                 final=(i == depth - 1))
    y_prompt = x[:N_TOK_P].reshape(BATCH, SEQ, D_MODEL)
    y_sample = x[N_TOK_P:].reshape(DEC_BATCH, 1, D_MODEL)
    ev = [jnp.stack([st[k] for st in even_states]) for k in range(10)]
    od = [jnp.stack([st[k] for st in odd_states]) for k in range(4)]
    return (y_prompt, y_sample, *ev, *od)
```
